```python
import jax, jax.numpy as jnp
from jax import lax
import numpy as np

D_MODEL = 2048
BATCH = 4
SEQ = 4096
DEPTH = 4

D_MIX = D_MODEL
RET_HEAD_DIM = D_MODEL // 16
RET_HEADS = 6
RET_WIDTH = RET_HEADS * RET_HEAD_DIM
RET_CHUNK = 128
ROPE_BASE = 10000.0
POOL_WINDOWS = (2, 4, 8, 16)
POOL_GROUPS = 4
POOL_GROUP_DIM = D_MODEL // 16
POOL_WIDTH = POOL_GROUPS * POOL_GROUP_DIM
GLA_HEADS = 4
GLA_WIDTH = D_MIX - RET_WIDTH - POOL_WIDTH
GLA_V_DIM = GLA_WIDTH // GLA_HEADS
GLA_K_DIM = GLA_V_DIM // 2
GLA_KEY_WIDTH = GLA_HEADS * GLA_K_DIM
GLA_GATE_RANK = 16
GLA_GATE_TAU = 16.0
GLA_CHUNK = 64
IN_WIDTH = 4 * RET_WIDTH + POOL_WIDTH + 2 * GLA_KEY_WIDTH + 2 * GLA_WIDTH + GLA_GATE_RANK
D_FF = 4 * D_MODEL
N_MOD = 6
EPS = 1e-6

kernel_name = "hybrid_retention_pool_gla_block"


def rms_norm(x, g):
    xf = x.astype(jnp.float32)
    y = xf * lax.rsqrt(jnp.mean(jnp.square(xf), axis=-1, keepdims=True) + EPS)
    return (y * g.astype(jnp.float32)).astype(x.dtype)


def head_layer_norm(o):
    mu = jnp.mean(o, axis=-1, keepdims=True)
    var = jnp.mean(jnp.square(o - mu), axis=-1, keepdims=True)
    return (o - mu) * lax.rsqrt(var + EPS)


def head_rms_norm(o):
    return o * lax.rsqrt(jnp.mean(jnp.square(o), axis=-1, keepdims=True) + EPS)


def rotary(x, cos, sin):
    x1, x2 = jnp.split(x, 2, axis=-1)
    c = cos[None, :, None, :]
    s = sin[None, :, None, :]
    return jnp.concatenate([x1 * c - x2 * s, x1 * s + x2 * c], axis=-1)


def retention(q, k, v):
    B, S, H, Dh = q.shape
    C = RET_CHUNK
    N = S // C
    log_g = jnp.log1p(-jnp.exp2(-5.0 - jnp.arange(H, dtype=jnp.float32)))
    q = q.reshape(B, N, C, H, Dh) * (Dh ** -0.5)
    k = k.reshape(B, N, C, H, Dh)
    v = v.reshape(B, N, C, H, Dh)
    idx = jnp.arange(C, dtype=jnp.float32)
    rel = idx[:, None] - idx[None, :]
    decay = jnp.where(rel[None] >= 0,
                      jnp.exp(jnp.maximum(rel, 0.0)[None] * log_g[:, None, None]), 0.0)
    scores = jnp.einsum('bnihd,bnjhd->bnhij', q, k) * decay
    inner = jnp.einsum('bnhij,bnjhe->bnihe', scores, v)
    zeta = jnp.exp((C - 1.0 - idx)[None, :] * log_g[:, None])
    u = jnp.einsum('bnjhd,hj,bnjhe->nbhde', k, zeta, v)
    g_chunk = jnp.exp(C * log_g)[None, :, None, None]

    def step(state, u_n):
        return g_chunk * state + u_n, state

    _, r_prev = lax.scan(step, jnp.zeros((B, H, Dh, Dh), jnp.float32), u)
    xi = jnp.exp((idx + 1.0)[None, :] * log_g[:, None])
    cross = jnp.einsum('bnihd,hi,nbhde->bnihe', q, xi, r_prev)
    return (inner + cross).reshape(B, S, H, Dh)


def gated_linear_attention(q, k, v, log_a):
    B, S, H, Dk = q.shape
    Dv = v.shape[-1]
    C = GLA_CHUNK
    N = S // C
    q = q.reshape(B, N, C, H, Dk) * (Dk ** -0.5)
    k = k.reshape(B, N, C, H, Dk)
    v = v.reshape(B, N, C, H, Dv)
    b = jnp.cumsum(log_a.reshape(B, N, C, H, Dk), axis=2)
    b_last = b[:, :, -1:]
    q_dec = q * jnp.exp(b)
    k_inv = k * jnp.exp(-b)
    causal = jnp.tril(jnp.ones((C, C), dtype=bool))
    scores = jnp.where(causal, jnp.einsum('bnihd,bnjhd->bnhij', q_dec, k_inv), 0.0)
    inner = jnp.einsum('bnhij,bnjhe->bnihe', scores, v)
    k_state = k * jnp.exp(b_last - b)
    u = jnp.einsum('bnjhd,bnjhe->nbhde', k_state, v)
    g_chunk = jnp.moveaxis(jnp.exp(b_last[:, :, 0]), 1, 0)[..., None]

    def step(state, inp):
        g_n, u_n = inp
        return g_n * state + u_n, state

    _, s_prev = lax.scan(step, jnp.zeros((B, H, Dk, Dv), jnp.float32), (g_chunk, u))
    cross = jnp.einsum('bnihd,nbhde->bnihe', q_dec, s_prev)
    return (inner + cross).reshape(B, S, H, Dv)


def multiscale_pool(u, w_pool, s_pool):
    B, S, _ = u.shape
    ug = u.reshape(B, S, POOL_GROUPS, POOL_GROUP_DIM)
    cs = jnp.concatenate([jnp.zeros((B, 1, POOL_GROUPS, POOL_GROUP_DIM), jnp.float32),
                          jnp.cumsum(ug, axis=1)], axis=1)
    t = jnp.arange(S)
    means = []
    for grp, w in enumerate(POOL_WINDOWS):
        lo = jnp.maximum(t + 1 - w, 0)
        win_sum = cs[:, 1:, grp] - jnp.take(cs[:, :, grp], lo, axis=1)
        cnt = (t + 1 - lo).astype(jnp.float32)
        means.append(win_sum / cnt[None, :, None])
    pooled = jnp.stack(means, axis=2) - ug
    y = jnp.einsum('bsgc,gcd->bsgd', pooled, w_pool.astype(jnp.float32))
    return y.reshape(B, S, POOL_WIDTH) * s_pool.astype(jnp.float32)


def hybrid_layer(x, c, w_ada, b_ada, g_mix, g_mlp, w_in, w_gate_up, b_gate,
                 w_pool, s_pool, w_out, w_up, w_down, cos, sin):
    B, S, _ = x.shape
    f32 = jnp.float32
    mod = jax.nn.silu(c) @ w_ada + b_ada
    sh1, sc1, gt1, sh2, sc2, gt2 = jnp.split(mod[:, None, :], N_MOD, axis=-1)

    h = rms_norm(x, g_mix) * (1.0 + sc1) + sh1
    z = h @ w_in
    sizes = [RET_WIDTH] * 4 + [POOL_WIDTH, GLA_KEY_WIDTH, GLA_KEY_WIDTH,
                               GLA_WIDTH, GLA_WIDTH, GLA_GATE_RANK]
    offsets = []
    acc = 0
    for sz in sizes[:-1]:
        acc += sz
        offsets.append(acc)
    (rq, rk, rv, rg, pu, gq, gk, gv, gr, gz) = jnp.split(z.astype(f32), offsets, axis=-1)

    rq = rotary(rq.reshape(B, S, RET_HEADS, RET_HEAD_DIM), cos, sin)
    rk = rotary(rk.reshape(B, S, RET_HEADS, RET_HEAD_DIM), cos, sin)
    ro = retention(rq, rk, rv.reshape(B, S, RET_HEADS, RET_HEAD_DIM))
    ret_out = head_layer_norm(ro).reshape(B, S, RET_WIDTH) * jax.nn.silu(rg)

    pool_out = multiscale_pool(pu, w_pool, s_pool)

    log_a = jax.nn.log_sigmoid(gz @ w_gate_up.astype(f32) + b_gate.astype(f32)) / GLA_GATE_TAU
    go = gated_linear_attention(gq.reshape(B, S, GLA_HEADS, GLA_K_DIM),
                                gk.reshape(B, S, GLA_HEADS, GLA_K_DIM),
                                gv.reshape(B, S, GLA_HEADS, GLA_V_DIM),
                                log_a.reshape(B, S, GLA_HEADS, GLA_K_DIM))
    gla_out = head_rms_norm(go).reshape(B, S, GLA_WIDTH) * jax.nn.silu(gr)

    mix = jnp.concatenate([ret_out, pool_out, gla_out], axis=-1).astype(x.dtype)
    x = x + gt1 * (mix @ w_out)

    h = rms_norm(x, g_mlp) * (1.0 + sc2) + sh2
    x = x + gt2 * (jnp.square(jax.nn.relu(h @ w_up)) @ w_down)
    return x


def setup_inputs(seed: int = 0) -> dict:
    key = jax.random.key(seed)
    ks = jax.random.split(key, 16)
    f32 = jnp.float32
    nrm = lambda k, shape, scale: jax.random.normal(k, shape, f32) * scale
    x = nrm(ks[0], (BATCH, SEQ, D_MODEL), 1.0)
    c = nrm(ks[1], (BATCH, D_MODEL), 1.0)
    w_ada = nrm(ks[2], (DEPTH, D_MODEL, N_MOD * D_MODEL), 0.5 * D_MODEL ** -0.5)
    b_ada = nrm(ks[3], (DEPTH, N_MOD * D_MODEL), 0.01)
    g_mix = 1.0 + nrm(ks[4], (DEPTH, D_MODEL), 0.05)
    g_mlp = 1.0 + nrm(ks[5], (DEPTH, D_MODEL), 0.05)
    w_in = nrm(ks[6], (DEPTH, D_MODEL, IN_WIDTH), D_MODEL ** -0.5)
    w_gate_up = nrm(ks[7], (DEPTH, GLA_GATE_RANK, GLA_KEY_WIDTH), GLA_GATE_RANK ** -0.5)
    b_gate = nrm(ks[8], (DEPTH, GLA_KEY_WIDTH), 0.1)
    w_pool = nrm(ks[9], (DEPTH, POOL_GROUPS, POOL_GROUP_DIM, POOL_GROUP_DIM), POOL_GROUP_DIM ** -0.5)
    s_pool = 1.0 + nrm(ks[10], (DEPTH, POOL_WIDTH), 0.1)
    w_out = nrm(ks[11], (DEPTH, D_MIX, D_MODEL), D_MIX ** -0.5)
    w_up = nrm(ks[12], (DEPTH, D_MODEL, D_FF), D_MODEL ** -0.5)
    w_down = nrm(ks[13], (DEPTH, D_FF, D_MODEL), D_FF ** -0.5)
    g_final = 1.0 + nrm(ks[14], (D_MODEL,), 0.05)
    return {"x": x, "c": c, "w_ada": w_ada, "b_ada": b_ada, "g_mix": g_mix,
            "g_mlp": g_mlp, "w_in": w_in, "w_gate_up": w_gate_up, "b_gate": b_gate,
            "w_pool": w_pool, "s_pool": s_pool, "w_out": w_out, "w_up": w_up,
            "w_down": w_down, "g_final": g_final}


def reference(x, c, w_ada, b_ada, g_mix, g_mlp, w_in, w_gate_up, b_gate,
              w_pool, s_pool, w_out, w_up, w_down, g_final):
    S = x.shape[1]
    pos = jnp.arange(S, dtype=jnp.float32)
    inv_freq = ROPE_BASE ** (-jnp.arange(0, RET_HEAD_DIM, 2, dtype=jnp.float32) / RET_HEAD_DIM)
    ang = pos[:, None] * inv_freq[None, :]
    cos, sin = jnp.cos(ang), jnp.sin(ang)
    for l in range(DEPTH):
        x = hybrid_layer(x, c, w_ada[l], b_ada[l], g_mix[l], g_mlp[l], w_in[l],
                         w_gate_up[l], b_gate[l], w_pool[l], s_pool[l], w_out[l],
                         w_up[l], w_down[l], cos, sin)
    return rms_norm(x, g_final)
```

```python
import functools

import jax
import jax.numpy as jnp
from jax import lax
from jax.experimental import pallas as pl
from jax.experimental.pallas import tpu as pltpu

F32 = jnp.float32
BF16 = jnp.bfloat16

D_MODEL = 2048
DEPTH = 4
N_MOD = 6
EPS = 1e-6
D_FF = 4 * D_MODEL

RET_HEADS = 6
RET_DIM = 128
RET_WIDTH = RET_HEADS * RET_DIM
RET_CHUNK = 128
ROPE_BASE = 10000.0
POOL_WINDOWS = (2, 4, 8, 16)
POOL_GROUPS = 4
POOL_DIM = 128
POOL_WIDTH = POOL_GROUPS * POOL_DIM
POOL_HALO = 16
GLA_HEADS = 4
GLA_V_DIM = 192
GLA_K_DIM = 96
GLA_WIDTH = GLA_HEADS * GLA_V_DIM
GLA_KEY_WIDTH = GLA_HEADS * GLA_K_DIM
GLA_RANK = 16
GLA_TAU = 16.0
GLA_CHUNK = 64

OFF_RQ = 0
OFF_RK = OFF_RQ + RET_WIDTH
OFF_RV = OFF_RK + RET_WIDTH
OFF_RG = OFF_RV + RET_WIDTH
OFF_PU = OFF_RG + RET_WIDTH
OFF_GQ = OFF_PU + POOL_WIDTH
OFF_GK = OFF_GQ + GLA_KEY_WIDTH
OFF_GV = OFF_GK + GLA_KEY_WIDTH
OFF_GR = OFF_GV + GLA_WIDTH
OFF_GZ = OFF_GR + GLA_WIDTH
IN_WIDTH = OFF_GZ + GLA_RANK
LANE = 128
Z_WIDTH = OFF_GZ + LANE

MIX_RET = 0
MIX_POOL = RET_WIDTH
MIX_GLA = RET_WIDTH + POOL_WIDTH

GLA_WIN = 256
GLA_KWIN = (0, 0, 128, 128)
GLA_VWIN = (0, 128, 384, 512)

VMEM_LIMIT = 56 * 1024 * 1024

MIX_CHUNK = 128


def _rms(x):
    return x * lax.rsqrt(jnp.mean(x * x, axis=-1, keepdims=True) + EPS)


def _silu(x):
    return x * jax.nn.sigmoid(x)


def _dot(a, b):
    return jnp.dot(a, b, preferred_element_type=F32)


def _dot_nt(a, b):
    return lax.dot_general(a, b, (((1,), (1,)), ((), ())), preferred_element_type=F32)


def _dot_tn(a, b):
    return lax.dot_general(a, b, (((0,), (0,)), ((), ())), preferred_element_type=F32)


def _ada_kernel(c_ref, w_ref, b_ref, o_ref):
    s = _silu(c_ref[...]).astype(BF16)
    o_ref[0] = _dot(s, w_ref[0].astype(BF16)) + b_ref[0]


def _ada(c_pad, w_ada, b_ada, tn=1024):
    depth, d, n = w_ada.shape
    rows = c_pad.shape[0]
    return pl.pallas_call(
        _ada_kernel,
        out_shape=jax.ShapeDtypeStruct((depth, rows, n), F32),
        grid=(depth, n // tn),
        in_specs=[
            pl.BlockSpec((rows, d), lambda l, j: (0, 0)),
            pl.BlockSpec((1, d, tn), lambda l, j: (l, 0, j)),
            pl.BlockSpec((1, 1, tn), lambda l, j: (l, 0, j)),
        ],
        out_specs=pl.BlockSpec((1, rows, tn), lambda l, j: (l, 0, j)),
        compiler_params=pltpu.CompilerParams(
            dimension_semantics=("arbitrary", "arbitrary"),
            vmem_limit_bytes=VMEM_LIMIT),
        name="ada_mod",
    )(c_pad, w_ada, b_ada.reshape(depth, 1, n))


def _inproj_kernel(x_ref, mod_ref, g_ref, w_ref, z_ref, *, col_chunk):
    x = x_ref[...]
    h = _rms(x) * g_ref[...]
    h = h * (1.0 + mod_ref[0, 1:2, :]) + mod_ref[0, 0:1, :]
    hb = h.astype(BF16)
    width = z_ref.shape[1]
    for c0 in range(0, width, col_chunk):
        c1 = min(c0 + col_chunk, width)
        z_ref[:, c0:c1] = _dot(hb, w_ref[:, c0:c1]).astype(BF16)


def _inproj(x2, mod, g, w, seq, tm=512, col_chunk=512):
    t, d = x2.shape
    zw = w.shape[1]
    per_batch = seq // tm
    return pl.pallas_call(
        functools.partial(_inproj_kernel, col_chunk=col_chunk),
        out_shape=jax.ShapeDtypeStruct((t, zw), BF16),
        grid=(t // tm,),
        in_specs=[
            pl.BlockSpec((tm, d), lambda i: (i, 0)),
            pl.BlockSpec((1, N_MOD, d), lambda i: (i // per_batch, 0, 0)),
            pl.BlockSpec((1, d), lambda i: (0, 0)),
            pl.BlockSpec((d, zw), lambda i: (0, 0), pipeline_mode=pl.Buffered(1)),
        ],
        out_specs=pl.BlockSpec((tm, zw), lambda i: (i, 0)),
        compiler_params=pltpu.CompilerParams(
            dimension_semantics=("arbitrary",),
            vmem_limit_bytes=VMEM_LIMIT),
        name="in_proj",
    )(x2, mod, g, w)


def _retention_chunk(z_ref, rows, cos, sin, dec_ref, zeta_ref, xi_ref, gc_ref,
                     r_state, mix_ref):
    for h in range(RET_HEADS):
        c0 = h * RET_DIM
        q = z_ref[rows, OFF_RQ + c0:OFF_RQ + c0 + RET_DIM].astype(F32)
        k = z_ref[rows, OFF_RK + c0:OFF_RK + c0 + RET_DIM].astype(F32)
        v = z_ref[rows, OFF_RV + c0:OFF_RV + c0 + RET_DIM]
        g = z_ref[rows, OFF_RG + c0:OFF_RG + c0 + RET_DIM].astype(F32)
        qr = q * cos + pltpu.roll(q, RET_DIM // 2, 1) * sin
        kr = k * cos + pltpu.roll(k, RET_DIM // 2, 1) * sin
        kb = kr.astype(BF16)
        scores = _dot_nt(qr.astype(BF16), kb) * dec_ref[h]
        state = r_state[h]
        o = _dot(scores.astype(BF16), v)
        o = o + _dot((qr * xi_ref[h]).astype(BF16), state.astype(BF16))
        vz = (v.astype(F32) * zeta_ref[h]).astype(BF16)
        r_state[h] = gc_ref[h] * state + _dot_tn(kb, vz)
        mu = jnp.mean(o, axis=-1, keepdims=True)
        oc = o - mu
        var = jnp.mean(oc * oc, axis=-1, keepdims=True)
        out = oc * lax.rsqrt(var + EPS) * _silu(g)
        mix_ref[rows, MIX_RET + c0:MIX_RET + c0 + RET_DIM] = out.astype(BF16)


def _pool_chunk(z_ref, rows, tok0, wp_ref, sp_ref, pext, mix_ref):
    n = MIX_CHUNK
    pu = z_ref[rows, OFF_PU:OFF_PU + POOL_WIDTH].astype(F32)
    pext[POOL_HALO:POOL_HALO + n, :] = pu
    tok = tok0 + lax.broadcasted_iota(jnp.int32, (n, 1), 0)
    for grp, win in enumerate(POOL_WINDOWS):
        c0 = grp * POOL_DIM
        cur = pu[:, c0:c0 + POOL_DIM]
        acc = cur
        for s in range(1, win):
            acc = acc + pext[POOL_HALO - s:POOL_HALO - s + n, c0:c0 + POOL_DIM]
        cnt = jnp.minimum(tok + 1, win).astype(F32)
        pooled = acc / cnt - cur
        y = _dot(pooled.astype(BF16), wp_ref[grp]) * sp_ref[:, c0:c0 + POOL_DIM]
        mix_ref[rows, MIX_POOL + c0:MIX_POOL + c0 + POOL_DIM] = y.astype(BF16)
    pext[0:POOL_HALO, :] = pext[n:n + POOL_HALO, :]


def _gla_chunk(z_ref, row0, rows, wg_ref, bg_ref, ltri_ref, s_state, mix_ref):
    gate = _dot(z_ref[rows, OFF_GZ:OFF_GZ + LANE], wg_ref[...]) + bg_ref[...]
    log_a = jax.nn.log_sigmoid(gate) / GLA_TAU
    la_hi = log_a.astype(BF16)
    rem = log_a - la_hi.astype(F32)
    la_mid = rem.astype(BF16)
    la_lo = (rem - la_mid.astype(F32)).astype(BF16)
    ltri = ltri_ref[...]
    b_all = _dot(ltri, la_hi) + _dot(ltri, la_mid) + _dot(ltri, la_lo)

    lane = lax.broadcasted_iota(jnp.int32, (1, GLA_WIN), 1)
    ri = lax.broadcasted_iota(jnp.int32, (GLA_CHUNK, GLA_CHUNK), 0)
    ci = lax.broadcasted_iota(jnp.int32, (GLA_CHUNK, GLA_CHUNK), 1)
    causal = ri >= ci
    scale = GLA_K_DIM ** -0.5

    for sub in range(MIX_CHUNK // GLA_CHUNK):
        r0 = sub * GLA_CHUNK
        srows = pl.ds(row0 + r0, GLA_CHUNK)
        b = b_all[r0:r0 + GLA_CHUNK, :]
        b_last = b[GLA_CHUNK - 1:GLA_CHUNK, :]
        q = z_ref[srows, OFF_GQ:OFF_GQ + GLA_KEY_WIDTH].astype(F32) * scale
        k = z_ref[srows, OFF_GK:OFF_GK + GLA_KEY_WIDTH].astype(F32)
        q_dec = q * jnp.exp(b)
        k_inv = k * jnp.exp(-b)
        k_state = k * jnp.exp(b_last - b)
        g_last = jnp.exp(b_last)
        outs = []
        for h in range(GLA_HEADS):
            k0 = GLA_KWIN[h]
            v0 = GLA_VWIN[h]
            kcol = lane + k0
            kmask = (kcol >= h * GLA_K_DIM) & (kcol < (h + 1) * GLA_K_DIM)
            vcol = lane + v0
            vmask = (vcol >= h * GLA_V_DIM) & (vcol < (h + 1) * GLA_V_DIM)
            qd = jnp.where(kmask, q_dec[:, k0:k0 + GLA_WIN], 0.0).astype(BF16)
            ki = jnp.where(kmask, k_inv[:, k0:k0 + GLA_WIN], 0.0).astype(BF16)
            ks = jnp.where(kmask, k_state[:, k0:k0 + GLA_WIN], 0.0).astype(BF16)
            v = z_ref[srows, OFF_GV + v0:OFF_GV + v0 + GLA_WIN]
            gr = z_ref[srows, OFF_GR + v0:OFF_GR + v0 + GLA_WIN].astype(F32)
            scores = jnp.where(causal, _dot_nt(qd, ki), 0.0)
            state = s_state[h]
            o = _dot(scores.astype(BF16), v) + _dot_nt(qd, state.astype(BF16))
            s_state[h] = state * g_last[:, k0:k0 + GLA_WIN] + _dot_tn(v, ks)
            ms = jnp.sum(jnp.where(vmask, o * o, 0.0), axis=-1, keepdims=True)
            on = o * lax.rsqrt(ms / GLA_V_DIM + EPS)
            outs.append(jnp.where(vmask, on * _silu(gr), 0.0))
        tiles = [
            outs[0][:, :LANE],
            outs[0][:, LANE:] + outs[1][:, :LANE],
            outs[1][:, LANE:],
            outs[2][:, :LANE],
            outs[2][:, LANE:] + outs[3][:, :LANE],
            outs[3][:, LANE:],
        ]
        for t, tile in enumerate(tiles):
            mix_ref[srows, MIX_GLA + t * LANE:MIX_GLA + (t + 1) * LANE] = tile.astype(BF16)


def _mixer_kernel(z_ref, cos_ref, sin_ref, dec_ref, zeta_ref, xi_ref, gc_ref,
                  wg_ref, bg_ref, wp_ref, sp_ref, ltri_ref, mix_ref,
                  r_state, s_state, pext):
    sblk = pl.program_id(1)
    tb = z_ref.shape[0]

    @pl.when(sblk == 0)
    def _():
        r_state[...] = jnp.zeros_like(r_state)
        s_state[...] = jnp.zeros_like(s_state)
        pext[...] = jnp.zeros_like(pext)

    def chunk(c, carry):
        row0 = pl.multiple_of(c * MIX_CHUNK, MIX_CHUNK)
        rows = pl.ds(row0, MIX_CHUNK)
        cos = cos_ref[rows, :]
        sin = sin_ref[rows, :]
        _retention_chunk(z_ref, rows, cos, sin, dec_ref, zeta_ref, xi_ref, gc_ref,
                         r_state, mix_ref)
        _pool_chunk(z_ref, rows, sblk * tb + row0, wp_ref, sp_ref, pext, mix_ref)
        _gla_chunk(z_ref, row0, rows, wg_ref, bg_ref, ltri_ref, s_state, mix_ref)
        return carry

    lax.fori_loop(0, tb // MIX_CHUNK, chunk, 0)


def _mixer(z, tables, wg, bg, wp, sp, batch, seq, tb=512):
    t, zw = z.shape
    nblk = seq // tb
    cos2, sin2, dec, zeta, xi, gc, ltri = tables
    full = lambda shape: pl.BlockSpec(shape, lambda b, s: (0,) * len(shape))
    return pl.pallas_call(
        _mixer_kernel,
        out_shape=jax.ShapeDtypeStruct((t, D_MODEL), BF16),
        grid=(batch, nblk),
        in_specs=[
            pl.BlockSpec((tb, zw), lambda b, s: (b * nblk + s, 0)),
            pl.BlockSpec((tb, RET_DIM), lambda b, s: (s, 0)),
            pl.BlockSpec((tb, RET_DIM), lambda b, s: (s, 0)),
            full(dec.shape), full(zeta.shape), full(xi.shape), full(gc.shape),
            full(wg.shape), full(bg.shape), full(wp.shape), full(sp.shape),
            full(ltri.shape),
        ],
        out_specs=pl.BlockSpec((tb, D_MODEL), lambda b, s: (b * nblk + s, 0)),
        scratch_shapes=[
            pltpu.VMEM((RET_HEADS, RET_DIM, RET_DIM), F32),
            pltpu.VMEM((GLA_HEADS, GLA_WIN, GLA_WIN), F32),
            pltpu.VMEM((POOL_HALO + MIX_CHUNK, POOL_WIDTH), F32),
        ],
        compiler_params=pltpu.CompilerParams(
            dimension_semantics=("arbitrary", "arbitrary"),
            vmem_limit_bytes=VMEM_LIMIT),
        name="token_mixer",
    )(z, cos2, sin2, dec, zeta, xi, gc, wg, bg, wp, sp, ltri)


def _outproj_kernel(mix_ref, x_ref, mod_ref, w_ref, o_ref):
    y = _dot(mix_ref[...], w_ref[...])
    o_ref[...] = x_ref[...] + mod_ref[0, 2:3, :] * y


def _outproj(mix, x2, mod, w, seq, tm=512):
    t, d = x2.shape
    per_batch = seq // tm
    return pl.pallas_call(
        _outproj_kernel,
        out_shape=jax.ShapeDtypeStruct((t, d), F32),
        grid=(t // tm,),
        in_specs=[
            pl.BlockSpec((tm, d), lambda i: (i, 0)),
            pl.BlockSpec((tm, d), lambda i: (i, 0)),
            pl.BlockSpec((1, N_MOD, d), lambda i: (i // per_batch, 0, 0)),
            pl.BlockSpec((d, d), lambda i: (0, 0), pipeline_mode=pl.Buffered(1)),
        ],
        out_specs=pl.BlockSpec((tm, d), lambda i: (i, 0)),
        compiler_params=pltpu.CompilerParams(
            dimension_semantics=("arbitrary",),
            vmem_limit_bytes=VMEM_LIMIT),
        name="out_proj",
    )(mix, x2, mod, w)


def _mlp_kernel(x_ref, mod_ref, g_ref, gf_ref, wu_ref, wd_ref, o_ref, h_scr, *,
                final_norm):
    f = pl.program_id(1)
    last = pl.num_programs(1) - 1

    @pl.when(f == 0)
    def _():
        h = _rms(x_ref[...]) * g_ref[...]
        h = h * (1.0 + mod_ref[0, 4:5, :]) + mod_ref[0, 3:4, :]
        h_scr[...] = h.astype(BF16)

    a = _dot(h_scr[...], wu_ref[...])
    r = jnp.square(jnp.maximum(a, 0.0)).astype(BF16)
    contrib = _dot(r, wd_ref[...])

    @pl.when(f == 0)
    def _():
        o_ref[...] = contrib

    @pl.when(f > 0)
    def _():
        o_ref[...] += contrib

    @pl.when(f == last)
    def _():
        xo = x_ref[...] + mod_ref[0, 5:6, :] * o_ref[...]
        if final_norm:
            xo = _rms(xo) * gf_ref[...]
        o_ref[...] = xo


def _mlp(x2, mod, g, gf, wu, wd, seq, final_norm, tm=512, tf=512):
    t, d = x2.shape
    ff = wu.shape[1]
    per_batch = seq // tm
    return pl.pallas_call(
        functools.partial(_mlp_kernel, final_norm=final_norm),
        out_shape=jax.ShapeDtypeStruct((t, d), F32),
        grid=(t // tm, ff // tf),
        in_specs=[
            pl.BlockSpec((tm, d), lambda i, f: (i, 0)),
            pl.BlockSpec((1, N_MOD, d), lambda i, f: (i // per_batch, 0, 0)),
            pl.BlockSpec((1, d), lambda i, f: (0, 0)),
            pl.BlockSpec((1, d), lambda i, f: (0, 0)),
            pl.BlockSpec((d, tf), lambda i, f: (0, f)),
            pl.BlockSpec((tf, d), lambda i, f: (f, 0)),
        ],
        out_specs=pl.BlockSpec((tm, d), lambda i, f: (i, 0)),
        scratch_shapes=[pltpu.VMEM((tm, d), BF16)],
        compiler_params=pltpu.CompilerParams(
            dimension_semantics=("arbitrary", "arbitrary"),
            vmem_limit_bytes=VMEM_LIMIT),
        name="mlp",
    )(x2, mod, g, gf, wu, wd)


def _tables(seq):
    pos = jnp.arange(seq, dtype=F32)
    inv_freq = ROPE_BASE ** (-jnp.arange(0, RET_DIM, 2, dtype=F32) / RET_DIM)
    ang = pos[:, None] * inv_freq[None, :]
    cos, sin = jnp.cos(ang), jnp.sin(ang)
    cos2 = jnp.concatenate([cos, cos], axis=-1)
    sin2 = jnp.concatenate([-sin, sin], axis=-1)

    c = RET_CHUNK
    log_g = jnp.log1p(-jnp.exp2(-5.0 - jnp.arange(RET_HEADS, dtype=F32)))
    idx = jnp.arange(c, dtype=F32)
    rel = idx[:, None] - idx[None, :]
    decay = jnp.where(rel[None] >= 0,
                      jnp.exp(jnp.maximum(rel, 0.0)[None] * log_g[:, None, None]), 0.0)
    scale = RET_DIM ** -0.5
    dec = decay * scale
    zeta = jnp.exp((c - 1.0 - idx)[None, :] * log_g[:, None])
    zeta = jnp.broadcast_to(zeta[:, :, None], (RET_HEADS, c, RET_DIM))
    xi = jnp.exp((idx + 1.0)[None, :] * log_g[:, None]) * scale
    xi = jnp.broadcast_to(xi[:, :, None], (RET_HEADS, c, RET_DIM))
    gc = jnp.broadcast_to(jnp.exp(c * log_g)[:, None, None], (RET_HEADS, 1, RET_DIM))

    r = jnp.arange(MIX_CHUNK)
    same = (r[:, None] // GLA_CHUNK) == (r[None, :] // GLA_CHUNK)
    ltri = jnp.where(same & (r[:, None] >= r[None, :]), 1.0, 0.0).astype(BF16)
    return cos2, sin2, dec, zeta, xi, gc, ltri


def kernel(x, c, w_ada, b_ada, g_mix, g_mlp, w_in, w_gate_up, b_gate, w_pool, s_pool,
           w_out, w_up, w_down, g_final):
    batch, seq, d = x.shape
    depth = w_ada.shape[0]
    tables = _tables(seq)

    c_pad = jnp.pad(c, ((0, 8 - batch), (0, 0)))
    mod_all = _ada(c_pad, w_ada, b_ada)[:, :batch].reshape(depth, batch, N_MOD, d)

    w_in_b = jnp.pad(w_in, ((0, 0), (0, 0), (0, Z_WIDTH - IN_WIDTH))).astype(BF16)
    w_out_b = w_out.astype(BF16)
    w_up_b = w_up.astype(BF16)
    w_down_b = w_down.astype(BF16)
    wg_b = jnp.pad(w_gate_up, ((0, 0), (0, LANE - GLA_RANK), (0, 0))).astype(BF16)
    wp_b = w_pool.astype(BF16)
    gf = g_final.reshape(1, d)

    x2 = x.reshape(batch * seq, d)
    for l in range(depth):
        mod = mod_all[l]
        z = _inproj(x2, mod, g_mix[l].reshape(1, d), w_in_b[l], seq)
        mix = _mixer(z, tables, wg_b[l], b_gate[l].reshape(1, -1), wp_b[l],
                     s_pool[l].reshape(1, -1), batch, seq)
        x2 = _outproj(mix, x2, mod, w_out_b[l], seq)
        x2 = _mlp(x2, mod, g_mlp[l].reshape(1, d), gf, w_up_b[l], w_down_b[l], seq,
                  final_norm=(l == depth - 1))
    return x2.reshape(batch, seq, d)
```

```python
import functools

import jax
import jax.numpy as jnp
from jax import lax
from jax.experimental import pallas as pl
from jax.experimental.pallas import tpu as pltpu

F32 = jnp.float32
BF16 = jnp.bfloat16

D_MODEL = 2048
DEPTH = 4
N_MOD = 6
EPS = 1e-6
D_FF = 4 * D_MODEL

RET_HEADS = 6
RET_DIM = 128
RET_WIDTH = RET_HEADS * RET_DIM
RET_CHUNK = 128
ROPE_BASE = 10000.0
POOL_WINDOWS = (2, 4, 8, 16)
POOL_GROUPS = 4
POOL_DIM = 128
POOL_WIDTH = POOL_GROUPS * POOL_DIM
POOL_HALO = 16
GLA_HEADS = 4
GLA_V_DIM = 192
GLA_K_DIM = 96
GLA_WIDTH = GLA_HEADS * GLA_V_DIM
GLA_KEY_WIDTH = GLA_HEADS * GLA_K_DIM
GLA_RANK = 16
GLA_TAU = 16.0
GLA_CHUNK = 64

OFF_RQ = 0
OFF_RK = OFF_RQ + RET_WIDTH
OFF_RV = OFF_RK + RET_WIDTH
OFF_RG = OFF_RV + RET_WIDTH
OFF_PU = OFF_RG + RET_WIDTH
OFF_GQ = OFF_PU + POOL_WIDTH
OFF_GK = OFF_GQ + GLA_KEY_WIDTH
OFF_GV = OFF_GK + GLA_KEY_WIDTH
OFF_GR = OFF_GV + GLA_WIDTH
OFF_GZ = OFF_GR + GLA_WIDTH
IN_WIDTH = OFF_GZ + GLA_RANK
LANE = 128
Z_WIDTH = OFF_GZ + LANE

MIX_RET = 0
MIX_POOL = RET_WIDTH
MIX_GLA = RET_WIDTH + POOL_WIDTH

GLA_WIN = 256
GLA_KWIN = (0, 0, 128, 128)
GLA_VWIN = (0, 128, 384, 512)

VMEM_LIMIT = 56 * 1024 * 1024

MIX_CHUNK = 128


def _rms(x):
    return x * lax.rsqrt(jnp.mean(x * x, axis=-1, keepdims=True) + EPS)


def _silu(x):
    return x * jax.nn.sigmoid(x)


def _dot(a, b):
    return jnp.dot(a, b, preferred_element_type=F32)


def _dot_nt(a, b):
    return lax.dot_general(a, b, (((1,), (1,)), ((), ())), preferred_element_type=F32)


def _dot_tn(a, b):
    return lax.dot_general(a, b, (((0,), (0,)), ((), ())), preferred_element_type=F32)


def _ada_kernel(c_ref, w_ref, b_ref, o_ref):
    s = _silu(c_ref[...]).astype(BF16)
    o_ref[0] = _dot(s, w_ref[0].astype(BF16)) + b_ref[0]


def _ada(c_pad, w_ada, b_ada, tn=1024):
    depth, d, n = w_ada.shape
    rows = c_pad.shape[0]
    return pl.pallas_call(
        _ada_kernel,
        out_shape=jax.ShapeDtypeStruct((depth, rows, n), F32),
        grid=(depth, n // tn),
        in_specs=[
            pl.BlockSpec((rows, d), lambda l, j: (0, 0)),
            pl.BlockSpec((1, d, tn), lambda l, j: (l, 0, j)),
            pl.BlockSpec((1, 1, tn), lambda l, j: (l, 0, j)),
        ],
        out_specs=pl.BlockSpec((1, rows, tn), lambda l, j: (l, 0, j)),
        compiler_params=pltpu.CompilerParams(
            dimension_semantics=("arbitrary", "arbitrary"),
            vmem_limit_bytes=VMEM_LIMIT),
        name="ada_mod",
    )(c_pad, w_ada, b_ada.reshape(depth, 1, n))


def _inproj_kernel(x_ref, mod_ref, g_ref, w_ref, z_ref, *, col_chunk):
    x = x_ref[...]
    h = _rms(x) * g_ref[...]
    h = h * (1.0 + mod_ref[0, 1:2, :]) + mod_ref[0, 0:1, :]
    hb = h.astype(BF16)
    width = z_ref.shape[1]
    for c0 in range(0, width, col_chunk):
        c1 = min(c0 + col_chunk, width)
        z_ref[:, c0:c1] = _dot(hb, w_ref[:, c0:c1]).astype(BF16)


def _inproj(x2, mod_all, g_all, w_all, layer, seq, tm=512, col_chunk=512):
    t, d = x2.shape
    zw = w_all.shape[2]
    per_batch = seq // tm
    return pl.pallas_call(
        functools.partial(_inproj_kernel, col_chunk=col_chunk),
        out_shape=jax.ShapeDtypeStruct((t, zw), BF16),
        grid=(t // tm,),
        in_specs=[
            pl.BlockSpec((tm, d), lambda i: (i, 0)),
            pl.BlockSpec((None, 1, N_MOD, d), lambda i: (layer, i // per_batch, 0, 0)),
            pl.BlockSpec((None, 1, d), lambda i: (layer, 0, 0)),
            pl.BlockSpec((None, d, zw), lambda i: (layer, 0, 0),
                         pipeline_mode=pl.Buffered(1)),
        ],
        out_specs=pl.BlockSpec((tm, zw), lambda i: (i, 0)),
        compiler_params=pltpu.CompilerParams(
            dimension_semantics=("arbitrary",),
            vmem_limit_bytes=VMEM_LIMIT),
        name="in_proj",
    )(x2, mod_all, g_all, w_all)


def _retention_chunk(z_ref, rows, cos, sin, dec_ref, zeta_ref, xi_ref, gc_ref,
                     r_state, mix_ref):
    for h in range(RET_HEADS):
        c0 = h * RET_DIM
        q = z_ref[rows, OFF_RQ + c0:OFF_RQ + c0 + RET_DIM].astype(F32)
        k = z_ref[rows, OFF_RK + c0:OFF_RK + c0 + RET_DIM].astype(F32)
        v = z_ref[rows, OFF_RV + c0:OFF_RV + c0 + RET_DIM]
        g = z_ref[rows, OFF_RG + c0:OFF_RG + c0 + RET_DIM].astype(F32)
        qr = q * cos + pltpu.roll(q, RET_DIM // 2, 1) * sin
        kr = k * cos + pltpu.roll(k, RET_DIM // 2, 1) * sin
        kb = kr.astype(BF16)
        scores = _dot_nt(qr.astype(BF16), kb) * dec_ref[h]
        state = r_state[h]
        o = _dot(scores.astype(BF16), v)
        o = o + _dot((qr * xi_ref[h]).astype(BF16), state.astype(BF16))
        vz = (v.astype(F32) * zeta_ref[h]).astype(BF16)
        r_state[h] = gc_ref[h] * state + _dot_tn(kb, vz)
        mu = jnp.mean(o, axis=-1, keepdims=True)
        oc = o - mu
        var = jnp.mean(oc * oc, axis=-1, keepdims=True)
        out = oc * lax.rsqrt(var + EPS) * _silu(g)
        mix_ref[rows, MIX_RET + c0:MIX_RET + c0 + RET_DIM] = out.astype(BF16)


def _pool_chunk(z_ref, rows, tok0, wp_ref, sp_ref, pext, mix_ref):
    n = MIX_CHUNK
    pu = z_ref[rows, OFF_PU:OFF_PU + POOL_WIDTH].astype(F32)
    pext[POOL_HALO:POOL_HALO + n, :] = pu
    tok = tok0 + lax.broadcasted_iota(jnp.int32, (n, 1), 0)
    for grp, win in enumerate(POOL_WINDOWS):
        c0 = grp * POOL_DIM
        cur = pu[:, c0:c0 + POOL_DIM]
        acc = cur
        for s in range(1, win):
            acc = acc + pext[POOL_HALO - s:POOL_HALO - s + n, c0:c0 + POOL_DIM]
        cnt = jnp.minimum(tok + 1, win).astype(F32)
        pooled = acc / cnt - cur
        y = _dot(pooled.astype(BF16), wp_ref[grp]) * sp_ref[:, c0:c0 + POOL_DIM]
        mix_ref[rows, MIX_POOL + c0:MIX_POOL + c0 + POOL_DIM] = y.astype(BF16)
    pext[0:POOL_HALO, :] = pext[n:n + POOL_HALO, :]


def _gla_chunk(z_ref, row0, rows, wg_ref, bg_ref, ltri_ref, s_state, mix_ref):
    gate = _dot(z_ref[rows, OFF_GZ:OFF_GZ + LANE], wg_ref[...]) + bg_ref[...]
    log_a = jax.nn.log_sigmoid(gate) / GLA_TAU
    la_hi = log_a.astype(BF16)
    rem = log_a - la_hi.astype(F32)
    la_mid = rem.astype(BF16)
    la_lo = (rem - la_mid.astype(F32)).astype(BF16)
    ltri = ltri_ref[...]
    b_all = _dot(ltri, la_hi) + _dot(ltri, la_mid) + _dot(ltri, la_lo)

    lane = lax.broadcasted_iota(jnp.int32, (1, GLA_WIN), 1)
    ri = lax.broadcasted_iota(jnp.int32, (GLA_CHUNK, GLA_CHUNK), 0)
    ci = lax.broadcasted_iota(jnp.int32, (GLA_CHUNK, GLA_CHUNK), 1)
    causal = ri >= ci
    scale = GLA_K_DIM ** -0.5

    for sub in range(MIX_CHUNK // GLA_CHUNK):
        r0 = sub * GLA_CHUNK
        srows = pl.ds(row0 + r0, GLA_CHUNK)
        b = b_all[r0:r0 + GLA_CHUNK, :]
        b_last = b[GLA_CHUNK - 1:GLA_CHUNK, :]
        q = z_ref[srows, OFF_GQ:OFF_GQ + GLA_KEY_WIDTH].astype(F32) * scale
        k = z_ref[srows, OFF_GK:OFF_GK + GLA_KEY_WIDTH].astype(F32)
        q_dec = q * jnp.exp(b)
        k_inv = k * jnp.exp(-b)
        k_state = k * jnp.exp(b_last - b)
        g_last = jnp.exp(b_last)
        outs = []
        for h in range(GLA_HEADS):
            k0 = GLA_KWIN[h]
            v0 = GLA_VWIN[h]
            kcol = lane + k0
            kmask = (kcol >= h * GLA_K_DIM) & (kcol < (h + 1) * GLA_K_DIM)
            vcol = lane + v0
            vmask = (vcol >= h * GLA_V_DIM) & (vcol < (h + 1) * GLA_V_DIM)
            qd = jnp.where(kmask, q_dec[:, k0:k0 + GLA_WIN], 0.0).astype(BF16)
            ki = jnp.where(kmask, k_inv[:, k0:k0 + GLA_WIN], 0.0).astype(BF16)
            ks = jnp.where(kmask, k_state[:, k0:k0 + GLA_WIN], 0.0).astype(BF16)
            v = z_ref[srows, OFF_GV + v0:OFF_GV + v0 + GLA_WIN]
            gr = z_ref[srows, OFF_GR + v0:OFF_GR + v0 + GLA_WIN].astype(F32)
            scores = jnp.where(causal, _dot_nt(qd, ki), 0.0)
            state = s_state[h]
            o = _dot(scores.astype(BF16), v) + _dot_nt(qd, state.astype(BF16))
            s_state[h] = state * g_last[:, k0:k0 + GLA_WIN] + _dot_tn(v, ks)
            ms = jnp.sum(jnp.where(vmask, o * o, 0.0), axis=-1, keepdims=True)
            on = o * lax.rsqrt(ms / GLA_V_DIM + EPS)
            outs.append(jnp.where(vmask, on * _silu(gr), 0.0))
        tiles = [
            outs[0][:, :LANE],
            outs[0][:, LANE:] + outs[1][:, :LANE],
            outs[1][:, LANE:],
            outs[2][:, :LANE],
            outs[2][:, LANE:] + outs[3][:, :LANE],
            outs[3][:, LANE:],
        ]
        for t, tile in enumerate(tiles):
            mix_ref[srows, MIX_GLA + t * LANE:MIX_GLA + (t + 1) * LANE] = tile.astype(BF16)


def _mixer_kernel(z_ref, cos_ref, sin_ref, dec_ref, zeta_ref, xi_ref, gc_ref,
                  wg_ref, bg_ref, wp_ref, sp_ref, ltri_ref, mix_ref,
                  r_state, s_state, pext):
    sblk = pl.program_id(1)
    tb = z_ref.shape[0]

    @pl.when(sblk == 0)
    def _():
        r_state[...] = jnp.zeros_like(r_state)
        s_state[...] = jnp.zeros_like(s_state)
        pext[...] = jnp.zeros_like(pext)

    def chunk(c, carry):
        row0 = pl.multiple_of(c * MIX_CHUNK, MIX_CHUNK)
        rows = pl.ds(row0, MIX_CHUNK)
        cos = cos_ref[rows, :]
        sin = sin_ref[rows, :]
        _retention_chunk(z_ref, rows, cos, sin, dec_ref, zeta_ref, xi_ref, gc_ref,
                         r_state, mix_ref)
        _pool_chunk(z_ref, rows, sblk * tb + row0, wp_ref, sp_ref, pext, mix_ref)
        _gla_chunk(z_ref, row0, rows, wg_ref, bg_ref, ltri_ref, s_state, mix_ref)
        return carry

    lax.fori_loop(0, tb // MIX_CHUNK, chunk, 0)


def _mixer(z, tables, wg, bg, wp, sp, layer, batch, seq, tb=512):
    t, zw = z.shape
    nblk = seq // tb
    cos2, sin2, dec, zeta, xi, gc, ltri = tables
    full = lambda shape: pl.BlockSpec(shape, lambda b, s: (0,) * len(shape))
    per_layer = lambda a: pl.BlockSpec(
        (None,) + a.shape[1:], lambda b, s: (layer,) + (0,) * (a.ndim - 1))
    return pl.pallas_call(
        _mixer_kernel,
        out_shape=jax.ShapeDtypeStruct((t, D_MODEL), BF16),
        grid=(batch, nblk),
        in_specs=[
            pl.BlockSpec((tb, zw), lambda b, s: (b * nblk + s, 0)),
            pl.BlockSpec((tb, RET_DIM), lambda b, s: (s, 0)),
            pl.BlockSpec((tb, RET_DIM), lambda b, s: (s, 0)),
            full(dec.shape), full(zeta.shape), full(xi.shape), full(gc.shape),
            per_layer(wg), per_layer(bg), per_layer(wp), per_layer(sp),
            full(ltri.shape),
        ],
        out_specs=pl.BlockSpec((tb, D_MODEL), lambda b, s: (b * nblk + s, 0)),
        scratch_shapes=[
            pltpu.VMEM((RET_HEADS, RET_DIM, RET_DIM), F32),
            pltpu.VMEM((GLA_HEADS, GLA_WIN, GLA_WIN), F32),
            pltpu.VMEM((POOL_HALO + MIX_CHUNK, POOL_WIDTH), F32),
        ],
        compiler_params=pltpu.CompilerParams(
            dimension_semantics=("arbitrary", "arbitrary"),
            vmem_limit_bytes=VMEM_LIMIT),
        name="token_mixer",
    )(z, cos2, sin2, dec, zeta, xi, gc, wg, bg, wp, sp, ltri)


def _outproj_kernel(mix_ref, x_ref, mod_ref, w_ref, o_ref):
    y = _dot(mix_ref[...], w_ref[...])
    o_ref[...] = x_ref[...] + mod_ref[0, 2:3, :] * y


def _outproj(mix, x2, mod_all, w_all, layer, seq, tm=512):
    t, d = x2.shape
    per_batch = seq // tm
    return pl.pallas_call(
        _outproj_kernel,
        out_shape=jax.ShapeDtypeStruct((t, d), F32),
        grid=(t // tm,),
        in_specs=[
            pl.BlockSpec((tm, d), lambda i: (i, 0)),
            pl.BlockSpec((tm, d), lambda i: (i, 0)),
            pl.BlockSpec((None, 1, N_MOD, d), lambda i: (layer, i // per_batch, 0, 0)),
            pl.BlockSpec((None, d, d), lambda i: (layer, 0, 0),
                         pipeline_mode=pl.Buffered(1)),
        ],
        out_specs=pl.BlockSpec((tm, d), lambda i: (i, 0)),
        compiler_params=pltpu.CompilerParams(
            dimension_semantics=("arbitrary",),
            vmem_limit_bytes=VMEM_LIMIT),
        name="out_proj",
    )(mix, x2, mod_all, w_all)


def _mlp_kernel(x_ref, mod_ref, g_ref, gf_ref, wu_ref, wd_ref, o_ref, h_scr, *,
                final_norm):
    f = pl.program_id(1)
    last = pl.num_programs(1) - 1

    @pl.when(f == 0)
    def _():
        h = _rms(x_ref[...]) * g_ref[...]
        h = h * (1.0 + mod_ref[0, 4:5, :]) + mod_ref[0, 3:4, :]
        h_scr[...] = h.astype(BF16)
        o_ref[...] = jnp.zeros_like(o_ref)

    a = _dot(h_scr[...], wu_ref[...])
    r = jnp.square(jnp.maximum(a, 0.0)).astype(BF16)
    o_ref[...] += _dot(r, wd_ref[...])

    @pl.when(f == last)
    def _():
        xo = x_ref[...] + mod_ref[0, 5:6, :] * o_ref[...]
        if final_norm:
            xo = _rms(xo) * gf_ref[...]
        o_ref[...] = xo


def _mlp(x2, mod_all, g_all, gf, wu_all, wd_all, layer, seq, final_norm, tm=512, tf=512):
    t, d = x2.shape
    ff = wu_all.shape[2]
    per_batch = seq // tm
    return pl.pallas_call(
        functools.partial(_mlp_kernel, final_norm=final_norm),
        out_shape=jax.ShapeDtypeStruct((t, d), F32),
        grid=(t // tm, ff // tf),
        in_specs=[
            pl.BlockSpec((tm, d), lambda i, f: (i, 0)),
            pl.BlockSpec((None, 1, N_MOD, d), lambda i, f: (layer, i // per_batch, 0, 0)),
            pl.BlockSpec((None, 1, d), lambda i, f: (layer, 0, 0)),
            pl.BlockSpec((1, d), lambda i, f: (0, 0)),
            pl.BlockSpec((None, d, tf), lambda i, f: (layer, 0, f)),
            pl.BlockSpec((None, tf, d), lambda i, f: (layer, f, 0)),
        ],
        out_specs=pl.BlockSpec((tm, d), lambda i, f: (i, 0)),
        scratch_shapes=[pltpu.VMEM((tm, d), BF16)],
        compiler_params=pltpu.CompilerParams(
            dimension_semantics=("arbitrary", "arbitrary"),
            vmem_limit_bytes=VMEM_LIMIT),
        name="mlp",
    )(x2, mod_all, g_all, gf, wu_all, wd_all)


def _tables(seq):
    pos = jnp.arange(seq, dtype=F32)
    inv_freq = ROPE_BASE ** (-jnp.arange(0, RET_DIM, 2, dtype=F32) / RET_DIM)
    ang = pos[:, None] * inv_freq[None, :]
    cos, sin = jnp.cos(ang), jnp.sin(ang)
    cos2 = jnp.concatenate([cos, cos], axis=-1)
    sin2 = jnp.concatenate([-sin, sin], axis=-1)

    c = RET_CHUNK
    log_g = jnp.log1p(-jnp.exp2(-5.0 - jnp.arange(RET_HEADS, dtype=F32)))
    idx = jnp.arange(c, dtype=F32)
    rel = idx[:, None] - idx[None, :]
    decay = jnp.where(rel[None] >= 0,
                      jnp.exp(jnp.maximum(rel, 0.0)[None] * log_g[:, None, None]), 0.0)
    scale = RET_DIM ** -0.5
    dec = decay * scale
    zeta = jnp.exp((c - 1.0 - idx)[None, :] * log_g[:, None])
    zeta = jnp.broadcast_to(zeta[:, :, None], (RET_HEADS, c, RET_DIM))
    xi = jnp.exp((idx + 1.0)[None, :] * log_g[:, None]) * scale
    xi = jnp.broadcast_to(xi[:, :, None], (RET_HEADS, c, RET_DIM))
    gc = jnp.broadcast_to(jnp.exp(c * log_g)[:, None, None], (RET_HEADS, 1, RET_DIM))

    r = jnp.arange(MIX_CHUNK)
    same = (r[:, None] // GLA_CHUNK) == (r[None, :] // GLA_CHUNK)
    ltri = jnp.where(same & (r[:, None] >= r[None, :]), 1.0, 0.0).astype(BF16)
    return cos2, sin2, dec, zeta, xi, gc, ltri


def kernel(x, c, w_ada, b_ada, g_mix, g_mlp, w_in, w_gate_up, b_gate, w_pool, s_pool,
           w_out, w_up, w_down, g_final):
    batch, seq, d = x.shape
    depth = w_ada.shape[0]
    tables = _tables(seq)

    c_pad = jnp.pad(c, ((0, 8 - batch), (0, 0)))
    mod_all = _ada(c_pad, w_ada, b_ada)[:, :batch].reshape(depth, batch, N_MOD, d)

    w_in_b = jnp.pad(w_in, ((0, 0), (0, 0), (0, Z_WIDTH - IN_WIDTH))).astype(BF16)
    w_out_b = w_out.astype(BF16)
    w_up_b = w_up.astype(BF16)
    w_down_b = w_down.astype(BF16)
    wg_b = jnp.pad(w_gate_up, ((0, 0), (0, LANE - GLA_RANK), (0, 0))).astype(BF16)
    wp_b = w_pool.astype(BF16)
    gf = g_final.reshape(1, d)
    g_mix3 = g_mix.reshape(depth, 1, d)
    g_mlp3 = g_mlp.reshape(depth, 1, d)
    bg3 = b_gate.reshape(depth, 1, -1)
    sp3 = s_pool.reshape(depth, 1, -1)

    x2 = x.reshape(batch * seq, d)
    for l in range(depth):
        z = _inproj(x2, mod_all, g_mix3, w_in_b, l, seq)
        mix = _mixer(z, tables, wg_b, bg3, wp_b, sp3, l, batch, seq)
        x2 = _outproj(mix, x2, mod_all, w_out_b, l, seq)
        x2 = _mlp(x2, mod_all, g_mlp3, gf, w_up_b, w_down_b, l, seq,
                  final_norm=(l == depth - 1))
    return x2.reshape(batch, seq, d)
```

```python
import functools

import jax
import jax.numpy as jnp
from jax import lax
from jax.experimental import pallas as pl
from jax.experimental.pallas import tpu as pltpu

F32 = jnp.float32
BF16 = jnp.bfloat16

D_MODEL = 2048
DEPTH = 4
N_MOD = 6
EPS = 1e-6
D_FF = 4 * D_MODEL

RET_HEADS = 6
RET_DIM = 128
RET_WIDTH = RET_HEADS * RET_DIM
RET_CHUNK = 128
ROPE_BASE = 10000.0
POOL_WINDOWS = (2, 4, 8, 16)
POOL_GROUPS = 4
POOL_DIM = 128
POOL_WIDTH = POOL_GROUPS * POOL_DIM
POOL_HALO = 16
GLA_HEADS = 4
GLA_V_DIM = 192
GLA_K_DIM = 96
GLA_WIDTH = GLA_HEADS * GLA_V_DIM
GLA_KEY_WIDTH = GLA_HEADS * GLA_K_DIM
GLA_RANK = 16
GLA_TAU = 16.0
GLA_CHUNK = 64

OFF_RQ = 0
OFF_RK = OFF_RQ + RET_WIDTH
OFF_RV = OFF_RK + RET_WIDTH
OFF_RG = OFF_RV + RET_WIDTH
OFF_PU = OFF_RG + RET_WIDTH
OFF_GQ = OFF_PU + POOL_WIDTH
OFF_GK = OFF_GQ + GLA_KEY_WIDTH
OFF_GV = OFF_GK + GLA_KEY_WIDTH
OFF_GR = OFF_GV + GLA_WIDTH
OFF_GZ = OFF_GR + GLA_WIDTH
IN_WIDTH = OFF_GZ + GLA_RANK
LANE = 128
Z_WIDTH = OFF_GZ + LANE

MIX_RET = 0
MIX_POOL = RET_WIDTH
MIX_GLA = RET_WIDTH + POOL_WIDTH

GLA_WIN = 256
GLA_KWIN = (0, 0, 128, 128)
GLA_KW = (256, 256, 256, 256)
GLA_VWIN = (0, 128, 384, 512)

VMEM_LIMIT = 56 * 1024 * 1024

MIX_CHUNK = 128
MLP_TF = 1024


def _rms(x):
    return x * lax.rsqrt(jnp.mean(x * x, axis=-1, keepdims=True) + EPS)


def _silu(x):
    return x * jax.nn.sigmoid(x)


def _dot(a, b):
    return jnp.dot(a, b, preferred_element_type=F32)


def _dot_nt(a, b):
    return lax.dot_general(a, b, (((1,), (1,)), ((), ())), preferred_element_type=F32)


def _dot_tn(a, b):
    return lax.dot_general(a, b, (((0,), (0,)), ((), ())), preferred_element_type=F32)


def _ada_kernel(c_ref, w_ref, b_ref, o_ref):
    k = pl.program_id(1)

    @pl.when(k == 0)
    def _():
        o_ref[...] = jnp.broadcast_to(b_ref[...], o_ref.shape)

    s = _silu(c_ref[...]).astype(BF16)
    o_ref[...] += _dot(s, w_ref[...].astype(BF16))


def _ada(c_pad, w_ada, b_ada, tk=128):
    depth, d, n = w_ada.shape
    rows = c_pad.shape[0]
    return pl.pallas_call(
        _ada_kernel,
        out_shape=jax.ShapeDtypeStruct((depth, rows, n), F32),
        grid=(depth, d // tk),
        in_specs=[
            pl.BlockSpec((rows, tk), lambda l, k: (0, k)),
            pl.BlockSpec((None, tk, n), lambda l, k: (l, k, 0)),
            pl.BlockSpec((None, 1, n), lambda l, k: (l, 0, 0)),
        ],
        out_specs=pl.BlockSpec((None, rows, n), lambda l, k: (l, 0, 0)),
        compiler_params=pltpu.CompilerParams(
            dimension_semantics=("arbitrary", "arbitrary"),
            vmem_limit_bytes=VMEM_LIMIT),
        name="ada_mod",
    )(c_pad, w_ada, b_ada.reshape(depth, 1, n))


def _inproj_kernel(x_ref, mod_ref, g_ref, w_ref, z_ref, *, col_chunk):
    x = x_ref[...]
    h = _rms(x) * g_ref[...]
    h = h * (1.0 + mod_ref[0, 1:2, :]) + mod_ref[0, 0:1, :]
    hb = h.astype(BF16)
    width = z_ref.shape[1]
    for c0 in range(0, width, col_chunk):
        c1 = min(c0 + col_chunk, width)
        z_ref[:, c0:c1] = _dot(hb, w_ref[:, c0:c1]).astype(BF16)


def _inproj(x2, mod_all, g_all, w_all, layer, seq, tm=512, col_chunk=512):
    t, d = x2.shape
    zw = w_all.shape[2]
    per_batch = seq // tm
    return pl.pallas_call(
        functools.partial(_inproj_kernel, col_chunk=col_chunk),
        out_shape=jax.ShapeDtypeStruct((t, zw), BF16),
        grid=(t // tm,),
        in_specs=[
            pl.BlockSpec((tm, d), lambda i: (i, 0)),
            pl.BlockSpec((None, 1, N_MOD, d), lambda i: (layer, i // per_batch, 0, 0)),
            pl.BlockSpec((None, 1, d), lambda i: (layer, 0, 0)),
            pl.BlockSpec((None, d, zw), lambda i: (layer, 0, 0),
                         pipeline_mode=pl.Buffered(1)),
        ],
        out_specs=pl.BlockSpec((tm, zw), lambda i: (i, 0)),
        compiler_params=pltpu.CompilerParams(
            dimension_semantics=("arbitrary",),
            vmem_limit_bytes=VMEM_LIMIT),
        name="in_proj",
    )(x2, mod_all, g_all, w_all)


def _retention_chunk(z_ref, rows, cos, sin, dec_ref, zeta_ref, xi_ref, gc_ref,
                     r_state, mix_ref):
    for h in range(RET_HEADS):
        c0 = h * RET_DIM
        q = z_ref[rows, OFF_RQ + c0:OFF_RQ + c0 + RET_DIM].astype(F32)
        k = z_ref[rows, OFF_RK + c0:OFF_RK + c0 + RET_DIM].astype(F32)
        v = z_ref[rows, OFF_RV + c0:OFF_RV + c0 + RET_DIM]
        g = z_ref[rows, OFF_RG + c0:OFF_RG + c0 + RET_DIM].astype(F32)
        qr = q * cos + pltpu.roll(q, RET_DIM // 2, 1) * sin
        kr = k * cos + pltpu.roll(k, RET_DIM // 2, 1) * sin
        kb = kr.astype(BF16)
        scores = _dot_nt(qr.astype(BF16), kb) * dec_ref[h]
        state = r_state[h]
        o = _dot(scores.astype(BF16), v)
        o = o + _dot((qr * xi_ref[h]).astype(BF16), state.astype(BF16))
        vz = (v.astype(F32) * zeta_ref[h]).astype(BF16)
        r_state[h] = gc_ref[h] * state + _dot_tn(kb, vz)
        mu = jnp.mean(o, axis=-1, keepdims=True)
        oc = o - mu
        var = jnp.mean(oc * oc, axis=-1, keepdims=True)
        out = oc * lax.rsqrt(var + EPS) * _silu(g)
        mix_ref[rows, MIX_RET + c0:MIX_RET + c0 + RET_DIM] = out.astype(BF16)


def _pool_chunk(z_ref, rows, tok0, wp_ref, sp_ref, pext, mix_ref):
    n = MIX_CHUNK
    pu = z_ref[rows, OFF_PU:OFF_PU + POOL_WIDTH].astype(F32)
    pext[POOL_HALO:POOL_HALO + n, :] = pu
    tok = tok0 + lax.broadcasted_iota(jnp.int32, (n, 1), 0)
    for grp, win in enumerate(POOL_WINDOWS):
        c0 = grp * POOL_DIM
        cur = pu[:, c0:c0 + POOL_DIM]
        acc = cur
        for s in range(1, win):
            acc = acc + pext[POOL_HALO - s:POOL_HALO - s + n, c0:c0 + POOL_DIM]
        cnt = jnp.minimum(tok + 1, win).astype(F32)
        pooled = acc / cnt - cur
        y = _dot(pooled.astype(BF16), wp_ref[grp]) * sp_ref[:, c0:c0 + POOL_DIM]
        mix_ref[rows, MIX_POOL + c0:MIX_POOL + c0 + POOL_DIM] = y.astype(BF16)
    pext[0:POOL_HALO, :] = pext[n:n + POOL_HALO, :]


def _gla_chunk(z_ref, row0, rows, wg_ref, bg_ref, ltri_ref, s_state, mix_ref):
    gate = _dot(z_ref[rows, OFF_GZ:OFF_GZ + LANE], wg_ref[...]) + bg_ref[...]
    log_a = jax.nn.log_sigmoid(gate) / GLA_TAU
    la_hi = log_a.astype(BF16)
    rem = log_a - la_hi.astype(F32)
    la_mid = rem.astype(BF16)
    la_lo = (rem - la_mid.astype(F32)).astype(BF16)
    ltri = ltri_ref[...]
    b_all = _dot(ltri, la_hi) + _dot(ltri, la_mid) + _dot(ltri, la_lo)

    lane = lax.broadcasted_iota(jnp.int32, (1, GLA_WIN), 1)
    ri = lax.broadcasted_iota(jnp.int32, (GLA_CHUNK, GLA_CHUNK), 0)
    ci = lax.broadcasted_iota(jnp.int32, (GLA_CHUNK, GLA_CHUNK), 1)
    causal = ri >= ci
    scale = GLA_K_DIM ** -0.5

    for sub in range(MIX_CHUNK // GLA_CHUNK):
        r0 = sub * GLA_CHUNK
        srows = pl.ds(row0 + r0, GLA_CHUNK)
        b = b_all[r0:r0 + GLA_CHUNK, :]
        b_last = b[GLA_CHUNK - 1:GLA_CHUNK, :]
        q = z_ref[srows, OFF_GQ:OFF_GQ + GLA_KEY_WIDTH].astype(F32) * scale
        k = z_ref[srows, OFF_GK:OFF_GK + GLA_KEY_WIDTH].astype(F32)
        q_dec = q * jnp.exp(b)
        k_inv = k * jnp.exp(-b)
        k_state = k * jnp.exp(b_last - b)
        g_last = jnp.exp(b_last)
        outs = []
        for h in range(GLA_HEADS):
            k0, kw = GLA_KWIN[h], GLA_KW[h]
            v0 = GLA_VWIN[h]
            kcol = lane[:, :kw] + k0
            kmask = (kcol >= h * GLA_K_DIM) & (kcol < (h + 1) * GLA_K_DIM)
            vcol = lane + v0
            vmask = (vcol >= h * GLA_V_DIM) & (vcol < (h + 1) * GLA_V_DIM)
            qd = jnp.where(kmask, q_dec[:, k0:k0 + kw], 0.0).astype(BF16)
            ki = jnp.where(kmask, k_inv[:, k0:k0 + kw], 0.0).astype(BF16)
            ks = jnp.where(kmask, k_state[:, k0:k0 + kw], 0.0).astype(BF16)
            v = z_ref[srows, OFF_GV + v0:OFF_GV + v0 + GLA_WIN]
            gr = z_ref[srows, OFF_GR + v0:OFF_GR + v0 + GLA_WIN].astype(F32)
            scores = jnp.where(causal, _dot_nt(qd, ki), 0.0)
            state = s_state[h, :, 0:kw]
            o = _dot(scores.astype(BF16), v) + _dot_nt(qd, state.astype(BF16))
            s_state[h, :, 0:kw] = state * g_last[:, k0:k0 + kw] + _dot_tn(v, ks)
            ms = jnp.sum(jnp.where(vmask, o * o, 0.0), axis=-1, keepdims=True)
            on = o * lax.rsqrt(ms / GLA_V_DIM + EPS)
            outs.append(jnp.where(vmask, on * _silu(gr), 0.0))
        tiles = [
            outs[0][:, :LANE],
            outs[0][:, LANE:] + outs[1][:, :LANE],
            outs[1][:, LANE:],
            outs[2][:, :LANE],
            outs[2][:, LANE:] + outs[3][:, :LANE],
            outs[3][:, LANE:],
        ]
        for t, tile in enumerate(tiles):
            mix_ref[srows, MIX_GLA + t * LANE:MIX_GLA + (t + 1) * LANE] = tile.astype(BF16)


def _mixer_kernel(z_ref, cos_ref, sin_ref, dec_ref, zeta_ref, xi_ref, gc_ref,
                  wg_ref, bg_ref, wp_ref, sp_ref, ltri_ref, mix_ref,
                  r_state, s_state, pext):
    sblk = pl.program_id(1)
    tb = z_ref.shape[0]

    @pl.when(sblk == 0)
    def _():
        r_state[...] = jnp.zeros_like(r_state)
        s_state[...] = jnp.zeros_like(s_state)
        pext[...] = jnp.zeros_like(pext)

    def chunk(c, carry):
        row0 = pl.multiple_of(c * MIX_CHUNK, MIX_CHUNK)
        rows = pl.ds(row0, MIX_CHUNK)
        cos = cos_ref[rows, :]
        sin = sin_ref[rows, :]
        _retention_chunk(z_ref, rows, cos, sin, dec_ref, zeta_ref, xi_ref, gc_ref,
                         r_state, mix_ref)
        _pool_chunk(z_ref, rows, sblk * tb + row0, wp_ref, sp_ref, pext, mix_ref)
        _gla_chunk(z_ref, row0, rows, wg_ref, bg_ref, ltri_ref, s_state, mix_ref)
        return carry

    lax.fori_loop(0, tb // MIX_CHUNK, chunk, 0)


def _mixer(z, tables, wg, bg, wp, sp, layer, batch, seq, tb=512):
    t, zw = z.shape
    nblk = seq // tb
    cos2, sin2, dec, zeta, xi, gc, ltri = tables
    full = lambda shape: pl.BlockSpec(shape, lambda b, s: (0,) * len(shape))
    per_layer = lambda a: pl.BlockSpec(
        (None,) + a.shape[1:], lambda b, s: (layer,) + (0,) * (a.ndim - 1))
    return pl.pallas_call(
        _mixer_kernel,
        out_shape=jax.ShapeDtypeStruct((t, D_MODEL), BF16),
        grid=(batch, nblk),
        in_specs=[
            pl.BlockSpec((tb, zw), lambda b, s: (b * nblk + s, 0)),
            pl.BlockSpec((tb, RET_DIM), lambda b, s: (s, 0)),
            pl.BlockSpec((tb, RET_DIM), lambda b, s: (s, 0)),
            full(dec.shape), full(zeta.shape), full(xi.shape), full(gc.shape),
            per_layer(wg), per_layer(bg), per_layer(wp), per_layer(sp),
            full(ltri.shape),
        ],
        out_specs=pl.BlockSpec((tb, D_MODEL), lambda b, s: (b * nblk + s, 0)),
        scratch_shapes=[
            pltpu.VMEM((RET_HEADS, RET_DIM, RET_DIM), F32),
            pltpu.VMEM((GLA_HEADS, GLA_WIN, GLA_WIN), F32),
            pltpu.VMEM((POOL_HALO + MIX_CHUNK, POOL_WIDTH), F32),
        ],
        compiler_params=pltpu.CompilerParams(
            dimension_semantics=("arbitrary", "arbitrary"),
            vmem_limit_bytes=VMEM_LIMIT),
        name="token_mixer",
    )(z, cos2, sin2, dec, zeta, xi, gc, wg, bg, wp, sp, ltri)


def _outproj_kernel(mix_ref, x_ref, mod_ref, w_ref, o_ref):
    y = _dot(mix_ref[...], w_ref[...])
    o_ref[...] = x_ref[...] + mod_ref[0, 2:3, :] * y


def _outproj(mix, x2, mod_all, w_all, layer, seq, tm=512):
    t, d = x2.shape
    per_batch = seq // tm
    return pl.pallas_call(
        _outproj_kernel,
        out_shape=jax.ShapeDtypeStruct((t, d), F32),
        grid=(t // tm,),
        in_specs=[
            pl.BlockSpec((tm, d), lambda i: (i, 0)),
            pl.BlockSpec((tm, d), lambda i: (i, 0)),
            pl.BlockSpec((None, 1, N_MOD, d), lambda i: (layer, i // per_batch, 0, 0)),
            pl.BlockSpec((None, d, d), lambda i: (layer, 0, 0),
                         pipeline_mode=pl.Buffered(1)),
        ],
        out_specs=pl.BlockSpec((tm, d), lambda i: (i, 0)),
        compiler_params=pltpu.CompilerParams(
            dimension_semantics=("arbitrary",),
            vmem_limit_bytes=VMEM_LIMIT),
        name="out_proj",
    )(mix, x2, mod_all, w_all)


def _mlp_norm(x, mod_ref, g_ref):
    h = _rms(x) * g_ref[...]
    return (h * (1.0 + mod_ref[0, 4:5, :]) + mod_ref[0, 3:4, :]).astype(BF16)


def _mlp_kernel(x_ref, xn_ref, mod_ref, modn_ref, g_ref, gf_ref, wu_ref, wd_ref, o_ref,
                h_scr, *, final_norm):
    i = pl.program_id(0)
    f = pl.program_id(1)
    nf = pl.num_programs(1)
    tm = x_ref.shape[0]
    rows_per_step = tm // nf
    cur = i % 2

    @pl.when((i == 0) & (f == 0))
    def _():
        h_scr[0] = _mlp_norm(x_ref[...], mod_ref, g_ref)

    @pl.when(f == 0)
    def _():
        o_ref[...] = jnp.zeros_like(o_ref)

    a = _dot(h_scr[cur], wu_ref[...])
    r = jnp.square(jnp.maximum(a, 0.0)).astype(BF16)
    o_ref[...] += _dot(r, wd_ref[...])

    r0 = pl.multiple_of(f * rows_per_step, rows_per_step)
    h_scr[1 - cur, pl.ds(r0, rows_per_step), :] = _mlp_norm(
        xn_ref[pl.ds(r0, rows_per_step), :], modn_ref, g_ref)

    @pl.when(f == nf - 1)
    def _():
        xo = x_ref[...] + mod_ref[0, 5:6, :] * o_ref[...]
        if final_norm:
            xo = _rms(xo) * gf_ref[...]
        o_ref[...] = xo


def _mlp(x2, mod_all, g_all, gf, wu_tiles, wd_all, layer, seq, final_norm, tm=512):
    t, d = x2.shape
    nf, tf = wu_tiles.shape[1], wu_tiles.shape[3]
    per_batch = seq // tm
    last_blk = t // tm - 1
    nxt = lambda i: jnp.minimum(i + 1, last_blk)
    return pl.pallas_call(
        functools.partial(_mlp_kernel, final_norm=final_norm),
        out_shape=jax.ShapeDtypeStruct((t, d), F32),
        grid=(t // tm, nf),
        in_specs=[
            pl.BlockSpec((tm, d), lambda i, f: (i, 0)),
            pl.BlockSpec((tm, d), lambda i, f: (nxt(i), 0)),
            pl.BlockSpec((None, 1, N_MOD, d), lambda i, f: (layer, i // per_batch, 0, 0)),
            pl.BlockSpec((None, 1, N_MOD, d),
                         lambda i, f: (layer, nxt(i) // per_batch, 0, 0)),
            pl.BlockSpec((None, 1, d), lambda i, f: (layer, 0, 0)),
            pl.BlockSpec((1, d), lambda i, f: (0, 0)),
            pl.BlockSpec((None, None, d, tf), lambda i, f: (layer, f, 0, 0)),
            pl.BlockSpec((None, tf, d), lambda i, f: (layer, f, 0)),
        ],
        out_specs=pl.BlockSpec((tm, d), lambda i, f: (i, 0)),
        scratch_shapes=[pltpu.VMEM((2, tm, d), BF16)],
        compiler_params=pltpu.CompilerParams(
            dimension_semantics=("arbitrary", "arbitrary"),
            vmem_limit_bytes=VMEM_LIMIT),
        name="mlp",
    )(x2, x2, mod_all, mod_all, g_all, gf, wu_tiles, wd_all)


def _tables(seq):
    pos = jnp.arange(seq, dtype=F32)
    inv_freq = ROPE_BASE ** (-jnp.arange(0, RET_DIM, 2, dtype=F32) / RET_DIM)
    ang = pos[:, None] * inv_freq[None, :]
    cos, sin = jnp.cos(ang), jnp.sin(ang)
    cos2 = jnp.concatenate([cos, cos], axis=-1)
    sin2 = jnp.concatenate([-sin, sin], axis=-1)

    c = RET_CHUNK
    log_g = jnp.log1p(-jnp.exp2(-5.0 - jnp.arange(RET_HEADS, dtype=F32)))
    idx = jnp.arange(c, dtype=F32)
    rel = idx[:, None] - idx[None, :]
    decay = jnp.where(rel[None] >= 0,
                      jnp.exp(jnp.maximum(rel, 0.0)[None] * log_g[:, None, None]), 0.0)
    scale = RET_DIM ** -0.5
    dec = decay * scale
    zeta = jnp.exp((c - 1.0 - idx)[None, :] * log_g[:, None])
    zeta = jnp.broadcast_to(zeta[:, :, None], (RET_HEADS, c, RET_DIM))
    xi = jnp.exp((idx + 1.0)[None, :] * log_g[:, None]) * scale
    xi = jnp.broadcast_to(xi[:, :, None], (RET_HEADS, c, RET_DIM))
    gc = jnp.broadcast_to(jnp.exp(c * log_g)[:, None, None], (RET_HEADS, 1, RET_DIM))

    r = jnp.arange(MIX_CHUNK)
    same = (r[:, None] // GLA_CHUNK) == (r[None, :] // GLA_CHUNK)
    ltri = jnp.where(same & (r[:, None] >= r[None, :]), 1.0, 0.0).astype(BF16)
    return cos2, sin2, dec, zeta, xi, gc, ltri


def kernel(x, c, w_ada, b_ada, g_mix, g_mlp, w_in, w_gate_up, b_gate, w_pool, s_pool,
           w_out, w_up, w_down, g_final):
    batch, seq, d = x.shape
    depth = w_ada.shape[0]
    tables = _tables(seq)

    c_pad = jnp.pad(c, ((0, 8 - batch), (0, 0)))
    mod_all = _ada(c_pad, w_ada, b_ada)[:, :batch].reshape(depth, batch, N_MOD, d)

    w_in_b = jnp.pad(w_in.astype(BF16), ((0, 0), (0, 0), (0, Z_WIDTH - IN_WIDTH)))
    w_out_b = w_out.astype(BF16)
    ff = w_up.shape[2]
    w_up_b = w_up.astype(BF16).reshape(depth, d, ff // MLP_TF, MLP_TF).transpose(0, 2, 1, 3)
    w_down_b = w_down.astype(BF16)
    wg_b = jnp.pad(w_gate_up, ((0, 0), (0, LANE - GLA_RANK), (0, 0))).astype(BF16)
    wp_b = w_pool.astype(BF16)
    gf = g_final.reshape(1, d)
    g_mix3 = g_mix.reshape(depth, 1, d)
    g_mlp3 = g_mlp.reshape(depth, 1, d)
    bg3 = b_gate.reshape(depth, 1, -1)
    sp3 = s_pool.reshape(depth, 1, -1)

    x2 = x.reshape(batch * seq, d)
    for l in range(depth):
        z = _inproj(x2, mod_all, g_mix3, w_in_b, l, seq)
        mix = _mixer(z, tables, wg_b, bg3, wp_b, sp3, l, batch, seq)
        x2 = _outproj(mix, x2, mod_all, w_out_b, l, seq)
        x2 = _mlp(x2, mod_all, g_mlp3, gf, w_up_b, w_down_b, l, seq,
                  final_norm=(l == depth - 1))
    return x2.reshape(batch, seq, d)
```

```python
import functools

import jax
import jax.numpy as jnp
from jax import lax
from jax.experimental import pallas as pl
from jax.experimental.pallas import tpu as pltpu

F32 = jnp.float32
BF16 = jnp.bfloat16

D_MODEL = 2048
DEPTH = 4
N_MOD = 6
EPS = 1e-6
D_FF = 4 * D_MODEL

RET_HEADS = 6
RET_DIM = 128
RET_WIDTH = RET_HEADS * RET_DIM
RET_CHUNK = 128
ROPE_BASE = 10000.0
POOL_WINDOWS = (2, 4, 8, 16)
POOL_GROUPS = 4
POOL_DIM = 128
POOL_WIDTH = POOL_GROUPS * POOL_DIM
POOL_HALO = 16
GLA_HEADS = 4
GLA_V_DIM = 192
GLA_K_DIM = 96
GLA_WIDTH = GLA_HEADS * GLA_V_DIM
GLA_KEY_WIDTH = GLA_HEADS * GLA_K_DIM
GLA_RANK = 16
GLA_TAU = 16.0
GLA_CHUNK = 64

OFF_RQ = 0
OFF_RK = OFF_RQ + RET_WIDTH
OFF_RV = OFF_RK + RET_WIDTH
OFF_RG = OFF_RV + RET_WIDTH
OFF_PU = OFF_RG + RET_WIDTH
OFF_GQ = OFF_PU + POOL_WIDTH
OFF_GK = OFF_GQ + GLA_KEY_WIDTH
OFF_GV = OFF_GK + GLA_KEY_WIDTH
OFF_GR = OFF_GV + GLA_WIDTH
OFF_GZ = OFF_GR + GLA_WIDTH
IN_WIDTH = OFF_GZ + GLA_RANK
LANE = 128
Z_WIDTH = OFF_GZ + LANE

MIX_RET = 0
MIX_POOL = RET_WIDTH
MIX_GLA = RET_WIDTH + POOL_WIDTH

GLA_WIN = 256
GLA_KWIN = (0, 0, 128, 128)
GLA_KW = (256, 256, 256, 256)
GLA_VWIN = (0, 128, 384, 512)

VMEM_LIMIT = 60 * 1024 * 1024

MIX_CHUNK = 128
MLP_TF = 1024


def _rms(x):
    return x * lax.rsqrt(jnp.mean(x * x, axis=-1, keepdims=True) + EPS)


def _silu(x):
    return x * jax.nn.sigmoid(x)


def _dot(a, b):
    return jnp.dot(a, b, preferred_element_type=F32)


def _dot_nt(a, b):
    return lax.dot_general(a, b, (((1,), (1,)), ((), ())), preferred_element_type=F32)


def _dot_tn(a, b):
    return lax.dot_general(a, b, (((0,), (0,)), ((), ())), preferred_element_type=F32)


def _ada_kernel(c_ref, wa_ref, wb_ref, b_ref, o_ref):
    k = pl.program_id(1)
    tk = wa_ref.shape[0]

    @pl.when(k == 0)
    def _():
        o_ref[...] = jnp.broadcast_to(b_ref[...], o_ref.shape)

    s = _silu(c_ref[...]).astype(BF16)
    o_ref[...] += (_dot(s[:, :tk], wa_ref[...].astype(BF16))
                   + _dot(s[:, tk:], wb_ref[...].astype(BF16)))


def _ada(c_pad, w_ada, b_ada, tk=128):
    depth, d, n = w_ada.shape
    rows = c_pad.shape[0]
    return pl.pallas_call(
        _ada_kernel,
        out_shape=jax.ShapeDtypeStruct((depth, rows, n), F32),
        grid=(depth, d // (2 * tk)),
        in_specs=[
            pl.BlockSpec((rows, 2 * tk), lambda l, k: (0, k)),
            pl.BlockSpec((None, tk, n), lambda l, k: (l, 2 * k, 0)),
            pl.BlockSpec((None, tk, n), lambda l, k: (l, 2 * k + 1, 0)),
            pl.BlockSpec((None, 1, n), lambda l, k: (l, 0, 0)),
        ],
        out_specs=pl.BlockSpec((None, rows, n), lambda l, k: (l, 0, 0)),
        compiler_params=pltpu.CompilerParams(
            dimension_semantics=("arbitrary", "arbitrary"),
            vmem_limit_bytes=VMEM_LIMIT),
        name="ada_mod",
    )(c_pad, w_ada, w_ada, b_ada.reshape(depth, 1, n))


def _inproj_kernel(x_ref, mod_ref, g_ref, w_ref, z_ref, *, col_chunk):
    x = x_ref[...]
    h = _rms(x) * g_ref[...]
    h = h * (1.0 + mod_ref[0, 1:2, :]) + mod_ref[0, 0:1, :]
    hb = h.astype(BF16)
    width = z_ref.shape[1]
    for c0 in range(0, width, col_chunk):
        c1 = min(c0 + col_chunk, width)
        z_ref[:, c0:c1] = _dot(hb, w_ref[:, c0:c1]).astype(BF16)


def _inproj(x2, mod_all, g_all, w_all, layer, seq, tm=512, col_chunk=512):
    t, d = x2.shape
    zw = w_all.shape[2]
    per_batch = seq // tm
    return pl.pallas_call(
        functools.partial(_inproj_kernel, col_chunk=col_chunk),
        out_shape=jax.ShapeDtypeStruct((t, zw), BF16),
        grid=(t // tm,),
        in_specs=[
            pl.BlockSpec((tm, d), lambda i: (i, 0)),
            pl.BlockSpec((None, 1, N_MOD, d), lambda i: (layer, i // per_batch, 0, 0)),
            pl.BlockSpec((None, 1, d), lambda i: (layer, 0, 0)),
            pl.BlockSpec((None, d, zw), lambda i: (layer, 0, 0),
                         pipeline_mode=pl.Buffered(1)),
        ],
        out_specs=pl.BlockSpec((tm, zw), lambda i: (i, 0)),
        compiler_params=pltpu.CompilerParams(
            dimension_semantics=("arbitrary",),
            vmem_limit_bytes=VMEM_LIMIT),
        name="in_proj",
    )(x2, mod_all, g_all, w_all)


def _retention_chunk(z_ref, rows, cos, sin, dec_ref, zeta_ref, xi_ref, gc_ref,
                     r_state, mix_ref):
    for h in range(RET_HEADS):
        c0 = h * RET_DIM
        q = z_ref[rows, OFF_RQ + c0:OFF_RQ + c0 + RET_DIM].astype(F32)
        k = z_ref[rows, OFF_RK + c0:OFF_RK + c0 + RET_DIM].astype(F32)
        v = z_ref[rows, OFF_RV + c0:OFF_RV + c0 + RET_DIM]
        g = z_ref[rows, OFF_RG + c0:OFF_RG + c0 + RET_DIM].astype(F32)
        qr = q * cos + pltpu.roll(q, RET_DIM // 2, 1) * sin
        kr = k * cos + pltpu.roll(k, RET_DIM // 2, 1) * sin
        kb = kr.astype(BF16)
        scores = _dot_nt(qr.astype(BF16), kb) * dec_ref[h]
        state = r_state[h]
        o = _dot(scores.astype(BF16), v)
        o = o + _dot((qr * xi_ref[h]).astype(BF16), state.astype(BF16))
        vz = (v.astype(F32) * zeta_ref[h]).astype(BF16)
        r_state[h] = gc_ref[h] * state + _dot_tn(kb, vz)
        mu = jnp.mean(o, axis=-1, keepdims=True)
        oc = o - mu
        var = jnp.mean(oc * oc, axis=-1, keepdims=True)
        out = oc * lax.rsqrt(var + EPS) * _silu(g)
        mix_ref[rows, MIX_RET + c0:MIX_RET + c0 + RET_DIM] = out.astype(BF16)


def _pool_chunk(z_ref, rows, tok0, wp_ref, sp_ref, pext, mix_ref):
    n = MIX_CHUNK
    pu = z_ref[rows, OFF_PU:OFF_PU + POOL_WIDTH].astype(F32)
    pext[POOL_HALO:POOL_HALO + n, :] = pu
    tok = tok0 + lax.broadcasted_iota(jnp.int32, (n, 1), 0)
    for grp, win in enumerate(POOL_WINDOWS):
        c0 = grp * POOL_DIM
        cur = pu[:, c0:c0 + POOL_DIM]
        acc = cur
        for s in range(1, win):
            acc = acc + pext[POOL_HALO - s:POOL_HALO - s + n, c0:c0 + POOL_DIM]
        cnt = jnp.minimum(tok + 1, win).astype(F32)
        pooled = acc / cnt - cur
        y = _dot(pooled.astype(BF16), wp_ref[grp]) * sp_ref[:, c0:c0 + POOL_DIM]
        mix_ref[rows, MIX_POOL + c0:MIX_POOL + c0 + POOL_DIM] = y.astype(BF16)
    pext[0:POOL_HALO, :] = pext[n:n + POOL_HALO, :]


def _gla_chunk(z_ref, row0, rows, wg_ref, bg_ref, ltri_ref, s_state, mix_ref):
    gate = _dot(z_ref[rows, OFF_GZ:OFF_GZ + LANE], wg_ref[...]) + bg_ref[...]
    log_a = jax.nn.log_sigmoid(gate) / GLA_TAU
    la_hi = log_a.astype(BF16)
    rem = log_a - la_hi.astype(F32)
    la_mid = rem.astype(BF16)
    la_lo = (rem - la_mid.astype(F32)).astype(BF16)
    ltri = ltri_ref[...]
    b_all = _dot(ltri, la_hi) + _dot(ltri, la_mid) + _dot(ltri, la_lo)

    lane = lax.broadcasted_iota(jnp.int32, (1, GLA_WIN), 1)
    ri = lax.broadcasted_iota(jnp.int32, (GLA_CHUNK, GLA_CHUNK), 0)
    ci = lax.broadcasted_iota(jnp.int32, (GLA_CHUNK, GLA_CHUNK), 1)
    causal = ri >= ci
    scale = GLA_K_DIM ** -0.5

    for sub in range(MIX_CHUNK // GLA_CHUNK):
        r0 = sub * GLA_CHUNK
        srows = pl.ds(row0 + r0, GLA_CHUNK)
        b = b_all[r0:r0 + GLA_CHUNK, :]
        b_last = b[GLA_CHUNK - 1:GLA_CHUNK, :]
        q = z_ref[srows, OFF_GQ:OFF_GQ + GLA_KEY_WIDTH].astype(F32) * scale
        k = z_ref[srows, OFF_GK:OFF_GK + GLA_KEY_WIDTH].astype(F32)
        q_dec = q * jnp.exp(b)
        k_inv = k * jnp.exp(-b)
        k_state = k * jnp.exp(b_last - b)
        g_last = jnp.exp(b_last)
        outs = []
        for h in range(GLA_HEADS):
            k0, kw = GLA_KWIN[h], GLA_KW[h]
            v0 = GLA_VWIN[h]
            kcol = lane[:, :kw] + k0
            kmask = (kcol >= h * GLA_K_DIM) & (kcol < (h + 1) * GLA_K_DIM)
            vcol = lane + v0
            vmask = (vcol >= h * GLA_V_DIM) & (vcol < (h + 1) * GLA_V_DIM)
            qd = jnp.where(kmask, q_dec[:, k0:k0 + kw], 0.0).astype(BF16)
            ki = jnp.where(kmask, k_inv[:, k0:k0 + kw], 0.0).astype(BF16)
            ks = jnp.where(kmask, k_state[:, k0:k0 + kw], 0.0).astype(BF16)
            v = z_ref[srows, OFF_GV + v0:OFF_GV + v0 + GLA_WIN]
            gr = z_ref[srows, OFF_GR + v0:OFF_GR + v0 + GLA_WIN].astype(F32)
            scores = jnp.where(causal, _dot_nt(qd, ki), 0.0)
            state = s_state[h, :, 0:kw]
            o = _dot(scores.astype(BF16), v) + _dot_nt(qd, state.astype(BF16))
            s_state[h, :, 0:kw] = state * g_last[:, k0:k0 + kw] + _dot_tn(v, ks)
            ms = jnp.sum(jnp.where(vmask, o * o, 0.0), axis=-1, keepdims=True)
            on = o * lax.rsqrt(ms / GLA_V_DIM + EPS)
            outs.append(jnp.where(vmask, on * _silu(gr), 0.0))
        tiles = [
            outs[0][:, :LANE],
            outs[0][:, LANE:] + outs[1][:, :LANE],
            outs[1][:, LANE:],
            outs[2][:, :LANE],
            outs[2][:, LANE:] + outs[3][:, :LANE],
            outs[3][:, LANE:],
        ]
        for t, tile in enumerate(tiles):
            mix_ref[srows, MIX_GLA + t * LANE:MIX_GLA + (t + 1) * LANE] = tile.astype(BF16)


def _mixer_kernel(z_ref, cos_ref, sin_ref, dec_ref, zeta_ref, xi_ref, gc_ref,
                  wg_ref, bg_ref, wp_ref, sp_ref, ltri_ref, mix_ref,
                  r_state, s_state, pext):
    sblk = pl.program_id(1)
    tb = z_ref.shape[0]

    @pl.when(sblk == 0)
    def _():
        r_state[...] = jnp.zeros_like(r_state)
        s_state[...] = jnp.zeros_like(s_state)
        pext[...] = jnp.zeros_like(pext)

    def chunk(c, carry):
        row0 = pl.multiple_of(c * MIX_CHUNK, MIX_CHUNK)
        rows = pl.ds(row0, MIX_CHUNK)
        cos = cos_ref[rows, :]
        sin = sin_ref[rows, :]
        _retention_chunk(z_ref, rows, cos, sin, dec_ref, zeta_ref, xi_ref, gc_ref,
                         r_state, mix_ref)
        _pool_chunk(z_ref, rows, sblk * tb + row0, wp_ref, sp_ref, pext, mix_ref)
        _gla_chunk(z_ref, row0, rows, wg_ref, bg_ref, ltri_ref, s_state, mix_ref)
        return carry

    lax.fori_loop(0, tb // MIX_CHUNK, chunk, 0)


def _mixer(z, tables, wg, bg, wp, sp, layer, batch, seq, tb=512):
    t, zw = z.shape
    nblk = seq // tb
    cos2, sin2, dec, zeta, xi, gc, ltri = tables
    full = lambda shape: pl.BlockSpec(shape, lambda b, s: (0,) * len(shape))
    per_layer = lambda a: pl.BlockSpec(
        (None,) + a.shape[1:], lambda b, s: (layer,) + (0,) * (a.ndim - 1))
    return pl.pallas_call(
        _mixer_kernel,
        out_shape=jax.ShapeDtypeStruct((t, D_MODEL), BF16),
        grid=(batch, nblk),
        in_specs=[
            pl.BlockSpec((tb, zw), lambda b, s: (b * nblk + s, 0)),
            pl.BlockSpec((tb, RET_DIM), lambda b, s: (s, 0)),
            pl.BlockSpec((tb, RET_DIM), lambda b, s: (s, 0)),
            full(dec.shape), full(zeta.shape), full(xi.shape), full(gc.shape),
            per_layer(wg), per_layer(bg), per_layer(wp), per_layer(sp),
            full(ltri.shape),
        ],
        out_specs=pl.BlockSpec((tb, D_MODEL), lambda b, s: (b * nblk + s, 0)),
        scratch_shapes=[
            pltpu.VMEM((RET_HEADS, RET_DIM, RET_DIM), F32),
            pltpu.VMEM((GLA_HEADS, GLA_WIN, GLA_WIN), F32),
            pltpu.VMEM((POOL_HALO + MIX_CHUNK, POOL_WIDTH), F32),
        ],
        compiler_params=pltpu.CompilerParams(
            dimension_semantics=("arbitrary", "arbitrary"),
            vmem_limit_bytes=VMEM_LIMIT),
        name="token_mixer",
    )(z, cos2, sin2, dec, zeta, xi, gc, wg, bg, wp, sp, ltri)


def _fused_step(x_ref, mod_ref, g_ref, w_ref, z_write, z_read, cos_ref, sin_ref,
                dec_ref, zeta_ref, xi_ref, gc_ref, wg_ref, bg_ref, wp_ref, sp_ref,
                ltri_ref, mix_ref, r_state, s_state, pext, tok_base, col_chunk):
    h = _rms(x_ref[...]) * g_ref[...]
    h = h * (1.0 + mod_ref[0, 1:2, :]) + mod_ref[0, 0:1, :]
    hb = h.astype(BF16)
    width = z_write.shape[1]
    for c0 in range(0, width, col_chunk):
        c1 = min(c0 + col_chunk, width)
        z_write[:, c0:c1] = _dot(hb, w_ref[:, c0:c1]).astype(BF16)

    for c in range(z_read.shape[0] // MIX_CHUNK):
        row0 = c * MIX_CHUNK
        rows = pl.ds(row0, MIX_CHUNK)
        cos = cos_ref[rows, :]
        sin = sin_ref[rows, :]
        _retention_chunk(z_read, rows, cos, sin, dec_ref, zeta_ref, xi_ref, gc_ref,
                         r_state, mix_ref)
        _pool_chunk(z_read, rows, tok_base + row0, wp_ref, sp_ref, pext, mix_ref)
        _gla_chunk(z_read, row0, rows, wg_ref, bg_ref, ltri_ref, s_state, mix_ref)


def _fused_kernel(x_ref, mod_ref, g_ref, w_ref, cos_ref, sin_ref, dec_ref, zeta_ref,
                  xi_ref, gc_ref, wg_ref, bg_ref, wp_ref, sp_ref, ltri_ref, mix_ref,
                  z_even, z_odd, r_state, s_state, pext, *, blocks_per_seq, col_chunk):
    j = pl.program_id(0)
    tb = x_ref.shape[0]
    sblk = lax.rem(jnp.maximum(j - 1, 0), blocks_per_seq)

    @pl.when(j == 0)
    def _():
        z_odd[...] = jnp.zeros_like(z_odd)

    @pl.when(sblk == 0)
    def _():
        r_state[...] = jnp.zeros_like(r_state)
        s_state[...] = jnp.zeros_like(s_state)
        pext[...] = jnp.zeros_like(pext)

    step = functools.partial(
        _fused_step, cos_ref=cos_ref, sin_ref=sin_ref, dec_ref=dec_ref,
        zeta_ref=zeta_ref, xi_ref=xi_ref, gc_ref=gc_ref, wg_ref=wg_ref, bg_ref=bg_ref,
        wp_ref=wp_ref, sp_ref=sp_ref, ltri_ref=ltri_ref, mix_ref=mix_ref,
        r_state=r_state, s_state=s_state, pext=pext, tok_base=sblk * tb,
        col_chunk=col_chunk)

    @pl.when(j % 2 == 0)
    def _():
        step(x_ref, mod_ref, g_ref, w_ref, z_even, z_odd)

    @pl.when(j % 2 == 1)
    def _():
        step(x_ref, mod_ref, g_ref, w_ref, z_odd, z_even)


def _inproj_mixer(x2, mod_all, g_all, w_all, tables, wg, bg, wp, sp, layer, seq,
                  tb=512, col_chunk=512):
    t, d = x2.shape
    zw = w_all.shape[2]
    nblk = t // tb
    per_seq = seq // tb
    cos2, sin2, dec, zeta, xi, gc, ltri = tables
    cur = lambda j: jnp.minimum(j, nblk - 1)
    prev = lambda j: jnp.maximum(j - 1, 0)
    const = lambda a: pl.BlockSpec(a.shape, lambda j: (0,) * a.ndim,
                                   pipeline_mode=pl.Buffered(1))
    per_layer = lambda a: pl.BlockSpec(
        (None,) + a.shape[1:], lambda j: (layer,) + (0,) * (a.ndim - 1),
        pipeline_mode=pl.Buffered(1))
    return pl.pallas_call(
        functools.partial(_fused_kernel, blocks_per_seq=per_seq, col_chunk=col_chunk),
        out_shape=jax.ShapeDtypeStruct((t, D_MODEL), BF16),
        grid=(nblk + 1,),
        in_specs=[
            pl.BlockSpec((tb, d), lambda j: (cur(j), 0)),
            pl.BlockSpec((None, 1, N_MOD, d), lambda j: (layer, cur(j) // per_seq, 0, 0)),
            pl.BlockSpec((None, 1, d), lambda j: (layer, 0, 0)),
            per_layer(w_all),
            pl.BlockSpec((tb, RET_DIM), lambda j: (prev(j) % per_seq, 0)),
            pl.BlockSpec((tb, RET_DIM), lambda j: (prev(j) % per_seq, 0)),
            const(dec), const(zeta), const(xi), const(gc),
            per_layer(wg), per_layer(bg), per_layer(wp), per_layer(sp),
            const(ltri),
        ],
        out_specs=pl.BlockSpec((tb, D_MODEL), lambda j: (prev(j), 0)),
        scratch_shapes=[
            pltpu.VMEM((tb, zw), BF16),
            pltpu.VMEM((tb, zw), BF16),
            pltpu.VMEM((RET_HEADS, RET_DIM, RET_DIM), F32),
            pltpu.VMEM((GLA_HEADS, GLA_WIN, GLA_WIN), F32),
            pltpu.VMEM((POOL_HALO + MIX_CHUNK, POOL_WIDTH), F32),
        ],
        compiler_params=pltpu.CompilerParams(
            dimension_semantics=("arbitrary",),
            vmem_limit_bytes=VMEM_LIMIT),
        name="inproj_mixer",
    )(x2, mod_all, g_all, w_all, cos2, sin2, dec, zeta, xi, gc, wg, bg, wp, sp, ltri)


def _outproj_kernel(mix_ref, x_ref, mod_ref, w_ref, o_ref):
    y = _dot(mix_ref[...], w_ref[...])
    o_ref[...] = x_ref[...] + mod_ref[0, 2:3, :] * y


def _outproj(mix, x2, mod_all, w_all, layer, seq, tm=512):
    t, d = x2.shape
    per_batch = seq // tm
    return pl.pallas_call(
        _outproj_kernel,
        out_shape=jax.ShapeDtypeStruct((t, d), F32),
        grid=(t // tm,),
        in_specs=[
            pl.BlockSpec((tm, d), lambda i: (i, 0)),
            pl.BlockSpec((tm, d), lambda i: (i, 0)),
            pl.BlockSpec((None, 1, N_MOD, d), lambda i: (layer, i // per_batch, 0, 0)),
            pl.BlockSpec((None, d, d), lambda i: (layer, 0, 0),
                         pipeline_mode=pl.Buffered(1)),
        ],
        out_specs=pl.BlockSpec((tm, d), lambda i: (i, 0)),
        compiler_params=pltpu.CompilerParams(
            dimension_semantics=("arbitrary",),
            vmem_limit_bytes=VMEM_LIMIT),
        name="out_proj",
    )(mix, x2, mod_all, w_all)


def _mlp_norm(x, mod_ref, g_ref):
    h = _rms(x) * g_ref[...]
    return (h * (1.0 + mod_ref[0, 4:5, :]) + mod_ref[0, 3:4, :]).astype(BF16)


def _mlp_kernel(x_ref, xn_ref, mod_ref, modn_ref, g_ref, gf_ref, wu_ref, wd_ref, o_ref,
                h_scr, *, final_norm):
    i = pl.program_id(0)
    f = pl.program_id(1)
    nf = pl.num_programs(1)
    tm = x_ref.shape[0]
    rows_per_step = tm // nf
    cur = i % 2

    @pl.when((i == 0) & (f == 0))
    def _():
        h_scr[0] = _mlp_norm(x_ref[...], mod_ref, g_ref)

    @pl.when(f == 0)
    def _():
        o_ref[...] = jnp.zeros_like(o_ref)

    a = _dot(h_scr[cur], wu_ref[...])
    r = jnp.square(jnp.maximum(a, 0.0)).astype(BF16)
    o_ref[...] += _dot(r, wd_ref[...])

    r0 = pl.multiple_of(f * rows_per_step, rows_per_step)
    h_scr[1 - cur, pl.ds(r0, rows_per_step), :] = _mlp_norm(
        xn_ref[pl.ds(r0, rows_per_step), :], modn_ref, g_ref)

    @pl.when(f == nf - 1)
    def _():
        xo = x_ref[...] + mod_ref[0, 5:6, :] * o_ref[...]
        if final_norm:
            xo = _rms(xo) * gf_ref[...]
        o_ref[...] = xo


def _mlp(x2, mod_all, g_all, gf, wu_all, wd_all, layer, seq, final_norm, tm=512,
         tf=MLP_TF):
    t, d = x2.shape
    nf = wu_all.shape[2] // tf
    per_batch = seq // tm
    last_blk = t // tm - 1
    nxt = lambda i: jnp.minimum(i + 1, last_blk)
    return pl.pallas_call(
        functools.partial(_mlp_kernel, final_norm=final_norm),
        out_shape=jax.ShapeDtypeStruct((t, d), F32),
        grid=(t // tm, nf),
        in_specs=[
            pl.BlockSpec((tm, d), lambda i, f: (i, 0)),
            pl.BlockSpec((tm, d), lambda i, f: (nxt(i), 0)),
            pl.BlockSpec((None, 1, N_MOD, d), lambda i, f: (layer, i // per_batch, 0, 0)),
            pl.BlockSpec((None, 1, N_MOD, d),
                         lambda i, f: (layer, nxt(i) // per_batch, 0, 0)),
            pl.BlockSpec((None, 1, d), lambda i, f: (layer, 0, 0)),
            pl.BlockSpec((1, d), lambda i, f: (0, 0)),
            pl.BlockSpec((None, d, tf), lambda i, f: (layer, 0, f)),
            pl.BlockSpec((None, tf, d), lambda i, f: (layer, f, 0)),
        ],
        out_specs=pl.BlockSpec((tm, d), lambda i, f: (i, 0)),
        scratch_shapes=[pltpu.VMEM((2, tm, d), BF16)],
        compiler_params=pltpu.CompilerParams(
            dimension_semantics=("arbitrary", "arbitrary"),
            vmem_limit_bytes=VMEM_LIMIT),
        name="mlp",
    )(x2, x2, mod_all, mod_all, g_all, gf, wu_all, wd_all)


def _tables(seq):
    pos = jnp.arange(seq, dtype=F32)
    inv_freq = ROPE_BASE ** (-jnp.arange(0, RET_DIM, 2, dtype=F32) / RET_DIM)
    ang = pos[:, None] * inv_freq[None, :]
    cos, sin = jnp.cos(ang), jnp.sin(ang)
    cos2 = jnp.concatenate([cos, cos], axis=-1)
    sin2 = jnp.concatenate([-sin, sin], axis=-1)

    c = RET_CHUNK
    log_g = jnp.log1p(-jnp.exp2(-5.0 - jnp.arange(RET_HEADS, dtype=F32)))
    idx = jnp.arange(c, dtype=F32)
    rel = idx[:, None] - idx[None, :]
    decay = jnp.where(rel[None] >= 0,
                      jnp.exp(jnp.maximum(rel, 0.0)[None] * log_g[:, None, None]), 0.0)
    scale = RET_DIM ** -0.5
    dec = decay * scale
    zeta = jnp.exp((c - 1.0 - idx)[None, :] * log_g[:, None])
    zeta = jnp.broadcast_to(zeta[:, :, None], (RET_HEADS, c, RET_DIM))
    xi = jnp.exp((idx + 1.0)[None, :] * log_g[:, None]) * scale
    xi = jnp.broadcast_to(xi[:, :, None], (RET_HEADS, c, RET_DIM))
    gc = jnp.broadcast_to(jnp.exp(c * log_g)[:, None, None], (RET_HEADS, 1, RET_DIM))

    r = jnp.arange(MIX_CHUNK)
    same = (r[:, None] // GLA_CHUNK) == (r[None, :] // GLA_CHUNK)
    ltri = jnp.where(same & (r[:, None] >= r[None, :]), 1.0, 0.0).astype(BF16)
    return cos2, sin2, dec, zeta, xi, gc, ltri


def kernel(x, c, w_ada, b_ada, g_mix, g_mlp, w_in, w_gate_up, b_gate, w_pool, s_pool,
           w_out, w_up, w_down, g_final):
    batch, seq, d = x.shape
    depth = w_ada.shape[0]
    tables = _tables(seq)

    c_pad = jnp.pad(c, ((0, 8 - batch), (0, 0)))
    mod_all = _ada(c_pad, w_ada, b_ada)[:, :batch].reshape(depth, batch, N_MOD, d)

    w_in_b = jnp.pad(w_in.astype(BF16), ((0, 0), (0, 0), (0, Z_WIDTH - IN_WIDTH)))
    w_out_b = w_out.astype(BF16)
    w_up_b = w_up.astype(BF16)
    w_down_b = w_down.astype(BF16)
    wg_b = jnp.pad(w_gate_up, ((0, 0), (0, LANE - GLA_RANK), (0, 0))).astype(BF16)
    wp_b = w_pool.astype(BF16)
    gf = g_final.reshape(1, d)
    g_mix3 = g_mix.reshape(depth, 1, d)
    g_mlp3 = g_mlp.reshape(depth, 1, d)
    bg3 = b_gate.reshape(depth, 1, -1)
    sp3 = s_pool.reshape(depth, 1, -1)

    x2 = x.reshape(batch * seq, d)
    for l in range(depth):
        mix = _inproj_mixer(x2, mod_all, g_mix3, w_in_b, tables, wg_b, bg3, wp_b, sp3, l, seq)
        x2 = _outproj(mix, x2, mod_all, w_out_b, l, seq)
        x2 = _mlp(x2, mod_all, g_mlp3, gf, w_up_b, w_down_b, l, seq,
                  final_norm=(l == depth - 1))
    return x2.reshape(batch, seq, d)
```

```python
import functools

import jax
import jax.numpy as jnp
from jax import lax
from jax.experimental import pallas as pl
from jax.experimental.pallas import tpu as pltpu

F32 = jnp.float32
BF16 = jnp.bfloat16

D_MODEL = 2048
DEPTH = 4
N_MOD = 6
EPS = 1e-6
D_FF = 4 * D_MODEL

RET_HEADS = 6
RET_DIM = 128
RET_WIDTH = RET_HEADS * RET_DIM
RET_CHUNK = 128
ROPE_BASE = 10000.0
POOL_WINDOWS = (2, 4, 8, 16)
POOL_GROUPS = 4
POOL_DIM = 128
POOL_WIDTH = POOL_GROUPS * POOL_DIM
POOL_HALO = 16
GLA_HEADS = 4
GLA_V_DIM = 192
GLA_K_DIM = 96
GLA_WIDTH = GLA_HEADS * GLA_V_DIM
GLA_KEY_WIDTH = GLA_HEADS * GLA_K_DIM
GLA_RANK = 16
GLA_TAU = 16.0
GLA_CHUNK = 64

LANE = 128

SRC_GQ = 4 * RET_WIDTH + POOL_WIDTH
SRC_GK = SRC_GQ + GLA_KEY_WIDTH
SRC_GV = SRC_GK + GLA_KEY_WIDTH
SRC_GZ = SRC_GV + 2 * GLA_WIDTH
IN_WIDTH = SRC_GZ + GLA_RANK

GLA_K_PAD = LANE
GLA_KEY_PAD_WIDTH = GLA_HEADS * GLA_K_PAD
OFF_RQ = 0
OFF_RK = OFF_RQ + RET_WIDTH
OFF_RV = OFF_RK + RET_WIDTH
OFF_RG = OFF_RV + RET_WIDTH
OFF_PU = OFF_RG + RET_WIDTH
OFF_GQ = OFF_PU + POOL_WIDTH
OFF_GK = OFF_GQ + GLA_KEY_PAD_WIDTH
OFF_GV = OFF_GK + GLA_KEY_PAD_WIDTH
OFF_GR = OFF_GV + GLA_WIDTH
OFF_GZ = OFF_GR + GLA_WIDTH
Z_WIDTH = OFF_GZ + LANE

MIX_RET = 0
MIX_POOL = RET_WIDTH
MIX_GLA = RET_WIDTH + POOL_WIDTH

GLA_WIN = 256
GLA_VWIN = (0, 128, 384, 512)

VMEM_LIMIT = 60 * 1024 * 1024

MIX_CHUNK = 128
MLP_TF = 1024


def _rms(x):
    return x * lax.rsqrt(jnp.mean(x * x, axis=-1, keepdims=True) + EPS)


def _silu(x):
    return x * jax.nn.sigmoid(x)


def _dot(a, b):
    return jnp.dot(a, b, preferred_element_type=F32)


def _dot_nt(a, b):
    return lax.dot_general(a, b, (((1,), (1,)), ((), ())), preferred_element_type=F32)


def _dot_tn(a, b):
    return lax.dot_general(a, b, (((0,), (0,)), ((), ())), preferred_element_type=F32)


def _ada_kernel(c_ref, wa_ref, wb_ref, b_ref, o_ref):
    k = pl.program_id(1)
    tk = wa_ref.shape[0]

    @pl.when(k == 0)
    def _():
        o_ref[...] = jnp.broadcast_to(b_ref[...], o_ref.shape)

    s = _silu(c_ref[...]).astype(BF16)
    o_ref[...] += (_dot(s[:, :tk], wa_ref[...].astype(BF16))
                   + _dot(s[:, tk:], wb_ref[...].astype(BF16)))


def _ada(c_pad, w_ada, b_ada, tk=128):
    depth, d, n = w_ada.shape
    rows = c_pad.shape[0]
    return pl.pallas_call(
        _ada_kernel,
        out_shape=jax.ShapeDtypeStruct((depth, rows, n), F32),
        grid=(depth, d // (2 * tk)),
        in_specs=[
            pl.BlockSpec((rows, 2 * tk), lambda l, k: (0, k)),
            pl.BlockSpec((None, tk, n), lambda l, k: (l, 2 * k, 0)),
            pl.BlockSpec((None, tk, n), lambda l, k: (l, 2 * k + 1, 0)),
            pl.BlockSpec((None, 1, n), lambda l, k: (l, 0, 0)),
        ],
        out_specs=pl.BlockSpec((None, rows, n), lambda l, k: (l, 0, 0)),
        compiler_params=pltpu.CompilerParams(
            dimension_semantics=("arbitrary", "arbitrary"),
            vmem_limit_bytes=VMEM_LIMIT),
        name="ada_mod",
    )(c_pad, w_ada, w_ada, b_ada.reshape(depth, 1, n))


def _inproj_kernel(x_ref, mod_ref, g_ref, w_ref, z_ref, *, col_chunk):
    x = x_ref[...]
    h = _rms(x) * g_ref[...]
    h = h * (1.0 + mod_ref[0, 1:2, :]) + mod_ref[0, 0:1, :]
    hb = h.astype(BF16)
    width = z_ref.shape[1]
    for c0 in range(0, width, col_chunk):
        c1 = min(c0 + col_chunk, width)
        z_ref[:, c0:c1] = _dot(hb, w_ref[:, c0:c1]).astype(BF16)


def _inproj(x2, mod_all, g_all, w_all, layer, seq, tm=512, col_chunk=512):
    t, d = x2.shape
    zw = w_all.shape[2]
    per_batch = seq // tm
    return pl.pallas_call(
        functools.partial(_inproj_kernel, col_chunk=col_chunk),
        out_shape=jax.ShapeDtypeStruct((t, zw), BF16),
        grid=(t // tm,),
        in_specs=[
            pl.BlockSpec((tm, d), lambda i: (i, 0)),
            pl.BlockSpec((None, 1, N_MOD, d), lambda i: (layer, i // per_batch, 0, 0)),
            pl.BlockSpec((None, 1, d), lambda i: (layer, 0, 0)),
            pl.BlockSpec((None, d, zw), lambda i: (layer, 0, 0),
                         pipeline_mode=pl.Buffered(1)),
        ],
        out_specs=pl.BlockSpec((tm, zw), lambda i: (i, 0)),
        compiler_params=pltpu.CompilerParams(
            dimension_semantics=("arbitrary",),
            vmem_limit_bytes=VMEM_LIMIT),
        name="in_proj",
    )(x2, mod_all, g_all, w_all)


def _retention_chunk(z_ref, rows, cos, sin, dec_ref, zeta_ref, xi_ref, gc_ref,
                     r_state, mix_ref):
    heads = range(RET_HEADS)
    cols = [slice(h * RET_DIM, (h + 1) * RET_DIM) for h in heads]
    qx, kb, scores, outs = [], [], [], []
    for h in heads:
        q = z_ref[rows, OFF_RQ + cols[h].start:OFF_RQ + cols[h].stop].astype(F32)
        k = z_ref[rows, OFF_RK + cols[h].start:OFF_RK + cols[h].stop].astype(F32)
        qr = q * cos + pltpu.roll(q, RET_DIM // 2, 1) * sin
        kr = k * cos + pltpu.roll(k, RET_DIM // 2, 1) * sin
        kb.append(kr.astype(BF16))
        qx.append((qr * xi_ref[h]).astype(BF16))
        scores.append(_dot_nt(qr.astype(BF16), kb[h]))
    for h in heads:
        v = z_ref[rows, OFF_RV + cols[h].start:OFF_RV + cols[h].stop]
        state = r_state[h]
        lhs = jnp.concatenate([(scores[h] * dec_ref[h]).astype(BF16), qx[h]], axis=1)
        rhs = jnp.concatenate([v, state.astype(BF16)], axis=0)
        outs.append(_dot(lhs, rhs))
        vz = (v.astype(F32) * zeta_ref[h]).astype(BF16)
        r_state[h] = gc_ref[h] * state + _dot_tn(kb[h], vz)
    for h in heads:
        g = z_ref[rows, OFF_RG + cols[h].start:OFF_RG + cols[h].stop].astype(F32)
        o = outs[h]
        mu = jnp.mean(o, axis=-1, keepdims=True)
        oc = o - mu
        var = jnp.mean(oc * oc, axis=-1, keepdims=True)
        out = oc * lax.rsqrt(var + EPS) * _silu(g)
        mix_ref[rows, MIX_RET + cols[h].start:MIX_RET + cols[h].stop] = out.astype(BF16)


def _pool_chunk(z_ref, rows, tok0, wp_ref, sp_ref, pext, mix_ref):
    n = MIX_CHUNK
    pu = z_ref[rows, OFF_PU:OFF_PU + POOL_WIDTH].astype(F32)
    pext[POOL_HALO:POOL_HALO + n, :] = pu
    tok = tok0 + lax.broadcasted_iota(jnp.int32, (n, 1), 0)
    for grp, win in enumerate(POOL_WINDOWS):
        c0 = grp * POOL_DIM
        cur = pu[:, c0:c0 + POOL_DIM]
        acc = cur
        for s in range(1, win):
            acc = acc + pext[POOL_HALO - s:POOL_HALO - s + n, c0:c0 + POOL_DIM]
        cnt = jnp.minimum(tok + 1, win).astype(F32)
        pooled = acc / cnt - cur
        y = _dot(pooled.astype(BF16), wp_ref[grp]) * sp_ref[:, c0:c0 + POOL_DIM]
        mix_ref[rows, MIX_POOL + c0:MIX_POOL + c0 + POOL_DIM] = y.astype(BF16)
    pext[0:POOL_HALO, :] = pext[n:n + POOL_HALO, :]


def _gla_chunk(z_ref, rows, wg_ref, bg_ref, ltri_ref, s_state, mix_ref):
    n, half = MIX_CHUNK, GLA_CHUNK
    gate = _dot(z_ref[rows, OFF_GZ:OFF_GZ + LANE], wg_ref[...]) + bg_ref[...]
    log_a = jax.nn.log_sigmoid(gate) / GLA_TAU
    la_hi = log_a.astype(BF16)
    rem = log_a - la_hi.astype(F32)
    la_mid = rem.astype(BF16)
    la_lo = (rem - la_mid.astype(F32)).astype(BF16)
    ltri = ltri_ref[...]
    b = _dot(ltri, la_hi) + _dot(ltri, la_mid) + _dot(ltri, la_lo)

    second = lax.broadcasted_iota(jnp.int32, (n, 1), 0) >= half
    b_last0 = b[half - 1:half, :]
    b_last1 = b[n - 1:n, :]
    g0 = jnp.exp(b_last0)
    g1 = jnp.exp(b_last1)
    q = z_ref[rows, OFF_GQ:OFF_GQ + GLA_KEY_PAD_WIDTH].astype(F32) * (GLA_K_DIM ** -0.5)
    k = z_ref[rows, OFF_GK:OFF_GK + GLA_KEY_PAD_WIDTH].astype(F32)
    q_dec = q * jnp.exp(b)
    k_inv = k * jnp.exp(-b)
    k_state = k * jnp.exp(jnp.where(second, b_last1, b_last0) - b)
    q_cross = jnp.where(second, q_dec * g0, q_dec)
    k_upd = jnp.where(second, k_state, k_state * g1)
    k_first = jnp.where(second, 0.0, k_state)
    g_both = g0 * g1

    ri = lax.broadcasted_iota(jnp.int32, (n, n), 0)
    ci = lax.broadcasted_iota(jnp.int32, (n, n), 1)
    intra = ((ri >= half) == (ci >= half)) & (ri >= ci)
    lane = lax.broadcasted_iota(jnp.int32, (1, GLA_WIN), 1)

    heads = range(GLA_HEADS)
    kcs = [slice(h * GLA_K_PAD, (h + 1) * GLA_K_PAD) for h in heads]
    both, cross, vals, outs = [], [], [], []
    for h in heads:
        kc = kcs[h]
        keys = jnp.concatenate([k_inv[:, kc].astype(BF16), k_first[:, kc].astype(BF16)],
                               axis=0)
        both.append(_dot_nt(q_dec[:, kc].astype(BF16), keys))
        cross.append(_dot_nt(q_cross[:, kc].astype(BF16), s_state[h].astype(BF16)))
    for h in heads:
        kc = kcs[h]
        v0 = GLA_VWIN[h]
        v = z_ref[rows, OFF_GV + v0:OFF_GV + v0 + GLA_WIN]
        scores = (jnp.where(intra, both[h][:, :n], 0.0)
                  + jnp.where(second, both[h][:, n:], 0.0))
        vals.append(_dot(scores.astype(BF16), v) + cross[h])
        s_state[h] = s_state[h] * g_both[:, kc] + _dot_tn(v, k_upd[:, kc].astype(BF16))
    for h in heads:
        v0 = GLA_VWIN[h]
        vcol = lane + v0
        vmask = (vcol >= h * GLA_V_DIM) & (vcol < (h + 1) * GLA_V_DIM)
        gr = z_ref[rows, OFF_GR + v0:OFF_GR + v0 + GLA_WIN].astype(F32)
        o = vals[h]
        ms = jnp.sum(jnp.where(vmask, o * o, 0.0), axis=-1, keepdims=True)
        on = o * lax.rsqrt(ms / GLA_V_DIM + EPS)
        outs.append(jnp.where(vmask, on * _silu(gr), 0.0))
    tiles = [
        outs[0][:, :LANE],
        outs[0][:, LANE:] + outs[1][:, :LANE],
        outs[1][:, LANE:],
        outs[2][:, :LANE],
        outs[2][:, LANE:] + outs[3][:, :LANE],
        outs[3][:, LANE:],
    ]
    for t, tile in enumerate(tiles):
        mix_ref[rows, MIX_GLA + t * LANE:MIX_GLA + (t + 1) * LANE] = tile.astype(BF16)


def _mixer_kernel(z_ref, cos_ref, sin_ref, dec_ref, zeta_ref, xi_ref, gc_ref,
                  wg_ref, bg_ref, wp_ref, sp_ref, ltri_ref, mix_ref,
                  r_state, s_state, pext):
    sblk = pl.program_id(1)
    tb = z_ref.shape[0]

    @pl.when(sblk == 0)
    def _():
        r_state[...] = jnp.zeros_like(r_state)
        s_state[...] = jnp.zeros_like(s_state)
        pext[...] = jnp.zeros_like(pext)

    def chunk(c, carry):
        row0 = pl.multiple_of(c * MIX_CHUNK, MIX_CHUNK)
        rows = pl.ds(row0, MIX_CHUNK)
        cos = cos_ref[rows, :]
        sin = sin_ref[rows, :]
        _retention_chunk(z_ref, rows, cos, sin, dec_ref, zeta_ref, xi_ref, gc_ref,
                         r_state, mix_ref)
        _pool_chunk(z_ref, rows, sblk * tb + row0, wp_ref, sp_ref, pext, mix_ref)
        _gla_chunk(z_ref, rows, wg_ref, bg_ref, ltri_ref, s_state, mix_ref)
        return carry

    lax.fori_loop(0, tb // MIX_CHUNK, chunk, 0)


def _mixer(z, tables, wg, bg, wp, sp, layer, batch, seq, tb=512):
    t, zw = z.shape
    nblk = seq // tb
    cos2, sin2, dec, zeta, xi, gc, ltri = tables
    full = lambda shape: pl.BlockSpec(shape, lambda b, s: (0,) * len(shape))
    per_layer = lambda a: pl.BlockSpec(
        (None,) + a.shape[1:], lambda b, s: (layer,) + (0,) * (a.ndim - 1))
    return pl.pallas_call(
        _mixer_kernel,
        out_shape=jax.ShapeDtypeStruct((t, D_MODEL), BF16),
        grid=(batch, nblk),
        in_specs=[
            pl.BlockSpec((tb, zw), lambda b, s: (b * nblk + s, 0)),
            pl.BlockSpec((tb, RET_DIM), lambda b, s: (s, 0)),
            pl.BlockSpec((tb, RET_DIM), lambda b, s: (s, 0)),
            full(dec.shape), full(zeta.shape), full(xi.shape), full(gc.shape),
            per_layer(wg), per_layer(bg), per_layer(wp), per_layer(sp),
            full(ltri.shape),
        ],
        out_specs=pl.BlockSpec((tb, D_MODEL), lambda b, s: (b * nblk + s, 0)),
        scratch_shapes=[
            pltpu.VMEM((RET_HEADS, RET_DIM, RET_DIM), F32),
            pltpu.VMEM((GLA_HEADS, GLA_WIN, GLA_K_PAD), F32),
            pltpu.VMEM((POOL_HALO + MIX_CHUNK, POOL_WIDTH), F32),
        ],
        compiler_params=pltpu.CompilerParams(
            dimension_semantics=("arbitrary", "arbitrary"),
            vmem_limit_bytes=VMEM_LIMIT),
        name="token_mixer",
    )(z, cos2, sin2, dec, zeta, xi, gc, wg, bg, wp, sp, ltri)


def _fused_step(x_ref, mod_ref, g_ref, w_ref, z_write, z_read, cos_ref, sin_ref,
                dec_ref, zeta_ref, xi_ref, gc_ref, wg_ref, bg_ref, wp_ref, sp_ref,
                ltri_ref, mix_ref, r_state, s_state, pext, tok_base, col_chunk):
    h = _rms(x_ref[...]) * g_ref[...]
    h = h * (1.0 + mod_ref[0, 1:2, :]) + mod_ref[0, 0:1, :]
    hb = h.astype(BF16)
    width = z_write.shape[1]
    for c0 in range(0, width, col_chunk):
        c1 = min(c0 + col_chunk, width)
        z_write[:, c0:c1] = _dot(hb, w_ref[:, c0:c1]).astype(BF16)

    for c in range(z_read.shape[0] // MIX_CHUNK):
        row0 = c * MIX_CHUNK
        rows = pl.ds(row0, MIX_CHUNK)
        cos = cos_ref[rows, :]
        sin = sin_ref[rows, :]
        _retention_chunk(z_read, rows, cos, sin, dec_ref, zeta_ref, xi_ref, gc_ref,
                         r_state, mix_ref)
        _pool_chunk(z_read, rows, tok_base + row0, wp_ref, sp_ref, pext, mix_ref)
        _gla_chunk(z_read, rows, wg_ref, bg_ref, ltri_ref, s_state, mix_ref)


def _fused_kernel(x_ref, mod_ref, g_ref, w_ref, cos_ref, sin_ref, dec_ref, zeta_ref,
                  xi_ref, gc_ref, wg_ref, bg_ref, wp_ref, sp_ref, ltri_ref, mix_ref,
                  z_even, z_odd, r_state, s_state, pext, *, blocks_per_seq, col_chunk):
    j = pl.program_id(0)
    tb = x_ref.shape[0]
    sblk = lax.rem(jnp.maximum(j - 1, 0), blocks_per_seq)

    @pl.when(j == 0)
    def _():
        z_odd[...] = jnp.zeros_like(z_odd)

    @pl.when(sblk == 0)
    def _():
        r_state[...] = jnp.zeros_like(r_state)
        s_state[...] = jnp.zeros_like(s_state)
        pext[...] = jnp.zeros_like(pext)

    step = functools.partial(
        _fused_step, cos_ref=cos_ref, sin_ref=sin_ref, dec_ref=dec_ref,
        zeta_ref=zeta_ref, xi_ref=xi_ref, gc_ref=gc_ref, wg_ref=wg_ref, bg_ref=bg_ref,
        wp_ref=wp_ref, sp_ref=sp_ref, ltri_ref=ltri_ref, mix_ref=mix_ref,
        r_state=r_state, s_state=s_state, pext=pext, tok_base=sblk * tb,
        col_chunk=col_chunk)

    @pl.when(j % 2 == 0)
    def _():
        step(x_ref, mod_ref, g_ref, w_ref, z_even, z_odd)

    @pl.when(j % 2 == 1)
    def _():
        step(x_ref, mod_ref, g_ref, w_ref, z_odd, z_even)


def _inproj_mixer(x2, mod_all, g_all, w_all, tables, wg, bg, wp, sp, layer, seq,
                  tb=512, col_chunk=512):
    t, d = x2.shape
    zw = w_all.shape[2]
    nblk = t // tb
    per_seq = seq // tb
    cos2, sin2, dec, zeta, xi, gc, ltri = tables
    cur = lambda j: jnp.minimum(j, nblk - 1)
    prev = lambda j: jnp.maximum(j - 1, 0)
    const = lambda a: pl.BlockSpec(a.shape, lambda j: (0,) * a.ndim,
                                   pipeline_mode=pl.Buffered(1))
    per_layer = lambda a: pl.BlockSpec(
        (None,) + a.shape[1:], lambda j: (layer,) + (0,) * (a.ndim - 1),
        pipeline_mode=pl.Buffered(1))
    return pl.pallas_call(
        functools.partial(_fused_kernel, blocks_per_seq=per_seq, col_chunk=col_chunk),
        out_shape=jax.ShapeDtypeStruct((t, D_MODEL), BF16),
        grid=(nblk + 1,),
        in_specs=[
            pl.BlockSpec((tb, d), lambda j: (cur(j), 0)),
            pl.BlockSpec((None, 1, N_MOD, d), lambda j: (layer, cur(j) // per_seq, 0, 0)),
            pl.BlockSpec((None, 1, d), lambda j: (layer, 0, 0)),
            per_layer(w_all),
            pl.BlockSpec((tb, RET_DIM), lambda j: (prev(j) % per_seq, 0)),
            pl.BlockSpec((tb, RET_DIM), lambda j: (prev(j) % per_seq, 0)),
            const(dec), const(zeta), const(xi), const(gc),
            per_layer(wg), per_layer(bg), per_layer(wp), per_layer(sp),
            const(ltri),
        ],
        out_specs=pl.BlockSpec((tb, D_MODEL), lambda j: (prev(j), 0)),
        scratch_shapes=[
            pltpu.VMEM((tb, zw), BF16),
            pltpu.VMEM((tb, zw), BF16),
            pltpu.VMEM((RET_HEADS, RET_DIM, RET_DIM), F32),
            pltpu.VMEM((GLA_HEADS, GLA_WIN, GLA_K_PAD), F32),
            pltpu.VMEM((POOL_HALO + MIX_CHUNK, POOL_WIDTH), F32),
        ],
        compiler_params=pltpu.CompilerParams(
            dimension_semantics=("arbitrary",),
            vmem_limit_bytes=VMEM_LIMIT),
        name="inproj_mixer",
    )(x2, mod_all, g_all, w_all, cos2, sin2, dec, zeta, xi, gc, wg, bg, wp, sp, ltri)


def _outproj_kernel(mix_ref, x_ref, mod_ref, w_ref, o_ref):
    y = _dot(mix_ref[...], w_ref[...])
    o_ref[...] = x_ref[...] + mod_ref[0, 2:3, :] * y


def _outproj(mix, x2, mod_all, w_all, layer, seq, tm=512):
    t, d = x2.shape
    per_batch = seq // tm
    return pl.pallas_call(
        _outproj_kernel,
        out_shape=jax.ShapeDtypeStruct((t, d), F32),
        grid=(t // tm,),
        in_specs=[
            pl.BlockSpec((tm, d), lambda i: (i, 0)),
            pl.BlockSpec((tm, d), lambda i: (i, 0)),
            pl.BlockSpec((None, 1, N_MOD, d), lambda i: (layer, i // per_batch, 0, 0)),
            pl.BlockSpec((None, d, d), lambda i: (layer, 0, 0),
                         pipeline_mode=pl.Buffered(1)),
        ],
        out_specs=pl.BlockSpec((tm, d), lambda i: (i, 0)),
        compiler_params=pltpu.CompilerParams(
            dimension_semantics=("arbitrary",),
            vmem_limit_bytes=VMEM_LIMIT),
        name="out_proj",
    )(mix, x2, mod_all, w_all)


def _mlp_norm(x, mod_ref, g_ref):
    h = _rms(x) * g_ref[...]
    return (h * (1.0 + mod_ref[0, 4:5, :]) + mod_ref[0, 3:4, :]).astype(BF16)


def _mlp_kernel(x_ref, xn_ref, mod_ref, modn_ref, g_ref, gf_ref, wu_ref, wd_ref, o_ref,
                h_scr, *, final_norm):
    i = pl.program_id(0)
    f = pl.program_id(1)
    nf = pl.num_programs(1)
    tm = x_ref.shape[0]
    rows_per_step = tm // nf
    cur = i % 2

    @pl.when((i == 0) & (f == 0))
    def _():
        h_scr[0] = _mlp_norm(x_ref[...], mod_ref, g_ref)

    def step(first, last):
        a = _dot(h_scr[cur], wu_ref[...])
        r = jnp.square(jnp.maximum(a, 0.0)).astype(BF16)
        y = _dot(r, wd_ref[...])
        if first:
            o_ref[...] = y
        elif not last:
            o_ref[...] += y
        else:
            xo = x_ref[...] + mod_ref[0, 5:6, :] * (o_ref[...] + y)
            if final_norm:
                xo = _rms(xo) * gf_ref[...]
            o_ref[...] = xo
        r0 = pl.multiple_of(f * rows_per_step, rows_per_step)
        h_scr[1 - cur, pl.ds(r0, rows_per_step), :] = _mlp_norm(
            xn_ref[pl.ds(r0, rows_per_step), :], modn_ref, g_ref)

    pl.when(f == 0)(functools.partial(step, True, False))
    pl.when((f > 0) & (f < nf - 1))(functools.partial(step, False, False))
    pl.when(f == nf - 1)(functools.partial(step, False, True))


def _mlp(x2, mod_all, g_all, gf, wu_all, wd_all, layer, seq, final_norm, tm=512,
         tf=MLP_TF):
    t, d = x2.shape
    nf = wu_all.shape[2] // tf
    assert nf >= 2, "the first and last hidden tiles take different code paths"
    per_batch = seq // tm
    last_blk = t // tm - 1
    nxt = lambda i: jnp.minimum(i + 1, last_blk)
    return pl.pallas_call(
        functools.partial(_mlp_kernel, final_norm=final_norm),
        out_shape=jax.ShapeDtypeStruct((t, d), F32),
        grid=(t // tm, nf),
        in_specs=[
            pl.BlockSpec((tm, d), lambda i, f: (i, 0)),
            pl.BlockSpec((tm, d), lambda i, f: (nxt(i), 0)),
            pl.BlockSpec((None, 1, N_MOD, d), lambda i, f: (layer, i // per_batch, 0, 0)),
            pl.BlockSpec((None, 1, N_MOD, d),
                         lambda i, f: (layer, nxt(i) // per_batch, 0, 0)),
            pl.BlockSpec((None, 1, d), lambda i, f: (layer, 0, 0)),
            pl.BlockSpec((1, d), lambda i, f: (0, 0)),
            pl.BlockSpec((None, d, tf), lambda i, f: (layer, 0, f)),
            pl.BlockSpec((None, tf, d), lambda i, f: (layer, f, 0)),
        ],
        out_specs=pl.BlockSpec((tm, d), lambda i, f: (i, 0)),
        scratch_shapes=[pltpu.VMEM((2, tm, d), BF16)],
        compiler_params=pltpu.CompilerParams(
            dimension_semantics=("arbitrary", "arbitrary"),
            vmem_limit_bytes=VMEM_LIMIT),
        name="mlp",
    )(x2, x2, mod_all, mod_all, g_all, gf, wu_all, wd_all)


def _tables(seq):
    pos = jnp.arange(seq, dtype=F32)
    inv_freq = ROPE_BASE ** (-jnp.arange(0, RET_DIM, 2, dtype=F32) / RET_DIM)
    ang = pos[:, None] * inv_freq[None, :]
    cos, sin = jnp.cos(ang), jnp.sin(ang)
    cos2 = jnp.concatenate([cos, cos], axis=-1)
    sin2 = jnp.concatenate([-sin, sin], axis=-1)

    c = RET_CHUNK
    log_g = jnp.log1p(-jnp.exp2(-5.0 - jnp.arange(RET_HEADS, dtype=F32)))
    idx = jnp.arange(c, dtype=F32)
    rel = idx[:, None] - idx[None, :]
    decay = jnp.where(rel[None] >= 0,
                      jnp.exp(jnp.maximum(rel, 0.0)[None] * log_g[:, None, None]), 0.0)
    scale = RET_DIM ** -0.5
    dec = decay * scale
    zeta = jnp.exp((c - 1.0 - idx)[None, :] * log_g[:, None])
    zeta = jnp.broadcast_to(zeta[:, :, None], (RET_HEADS, c, RET_DIM))
    xi = jnp.exp((idx + 1.0)[None, :] * log_g[:, None]) * scale
    xi = jnp.broadcast_to(xi[:, :, None], (RET_HEADS, c, RET_DIM))
    gc = jnp.broadcast_to(jnp.exp(c * log_g)[:, None, None], (RET_HEADS, 1, RET_DIM))

    r = jnp.arange(MIX_CHUNK)
    same = (r[:, None] // GLA_CHUNK) == (r[None, :] // GLA_CHUNK)
    ltri = jnp.where(same & (r[:, None] >= r[None, :]), 1.0, 0.0).astype(BF16)
    return cos2, sin2, dec, zeta, xi, gc, ltri


def _pad_key_heads(a):
    lead = a.shape[:-1]
    a = a.reshape(lead + (GLA_HEADS, GLA_K_DIM))
    a = jnp.pad(a, [(0, 0)] * (len(lead) + 1) + [(0, GLA_K_PAD - GLA_K_DIM)])
    return a.reshape(lead + (GLA_KEY_PAD_WIDTH,))


def _layout_in_columns(a):
    pad_rank = [(0, 0)] * (a.ndim - 1) + [(0, LANE - GLA_RANK)]
    return jnp.concatenate([
        a[..., :SRC_GQ],
        _pad_key_heads(a[..., SRC_GQ:SRC_GK]),
        _pad_key_heads(a[..., SRC_GK:SRC_GV]),
        a[..., SRC_GV:SRC_GZ],
        jnp.pad(a[..., SRC_GZ:], pad_rank),
    ], axis=-1)


def kernel(x, c, w_ada, b_ada, g_mix, g_mlp, w_in, w_gate_up, b_gate, w_pool, s_pool,
           w_out, w_up, w_down, g_final):
    batch, seq, d = x.shape
    depth = w_ada.shape[0]
    tables = _tables(seq)

    c_pad = jnp.pad(c, ((0, 8 - batch), (0, 0)))
    mod_all = _ada(c_pad, w_ada, b_ada)[:, :batch].reshape(depth, batch, N_MOD, d)

    w_in_b = _layout_in_columns(w_in.astype(BF16))
    w_out_b = w_out.astype(BF16)
    w_up_b = w_up.astype(BF16)
    w_down_b = w_down.astype(BF16)
    wg_b = jnp.pad(_pad_key_heads(w_gate_up),
                   ((0, 0), (0, LANE - GLA_RANK), (0, 0))).astype(BF16)
    wp_b = w_pool.astype(BF16)
    gf = g_final.reshape(1, d)
    g_mix3 = g_mix.reshape(depth, 1, d)
    g_mlp3 = g_mlp.reshape(depth, 1, d)
    bg3 = _pad_key_heads(b_gate).reshape(depth, 1, -1)
    sp3 = s_pool.reshape(depth, 1, -1)

    x2 = x.reshape(batch * seq, d)
    for l in range(depth):
        z = _inproj(x2, mod_all, g_mix3, w_in_b, l, seq)
        mix = _mixer(z, tables, wg_b, bg3, wp_b, sp3, l, batch, seq)
        x2 = _outproj(mix, x2, mod_all, w_out_b, l, seq)
        x2 = _mlp(x2, mod_all, g_mlp3, gf, w_up_b, w_down_b, l, seq,
                  final_norm=(l == depth - 1))
    return x2.reshape(batch, seq, d)
```

```python
import functools

import jax
import jax.numpy as jnp
from jax import lax
from jax.experimental import pallas as pl
from jax.experimental.pallas import tpu as pltpu

F32 = jnp.float32
BF16 = jnp.bfloat16

D_MODEL = 2048
DEPTH = 4
N_MOD = 6
EPS = 1e-6
D_FF = 4 * D_MODEL

RET_HEADS = 6
RET_DIM = 128
RET_WIDTH = RET_HEADS * RET_DIM
RET_CHUNK = 128
ROPE_BASE = 10000.0
POOL_WINDOWS = (2, 4, 8, 16)
POOL_GROUPS = 4
POOL_DIM = 128
POOL_WIDTH = POOL_GROUPS * POOL_DIM
POOL_HALO = 16
GLA_HEADS = 4
GLA_V_DIM = 192
GLA_K_DIM = 96
GLA_WIDTH = GLA_HEADS * GLA_V_DIM
GLA_KEY_WIDTH = GLA_HEADS * GLA_K_DIM
GLA_RANK = 16
GLA_TAU = 16.0
GLA_CHUNK = 64

LANE = 128

SRC_GQ = 4 * RET_WIDTH + POOL_WIDTH
SRC_GK = SRC_GQ + GLA_KEY_WIDTH
SRC_GV = SRC_GK + GLA_KEY_WIDTH
SRC_GZ = SRC_GV + 2 * GLA_WIDTH
IN_WIDTH = SRC_GZ + GLA_RANK

GLA_K_PAD = LANE
GLA_KEY_PAD_WIDTH = GLA_HEADS * GLA_K_PAD
OFF_RQ = 0
OFF_RK = OFF_RQ + RET_WIDTH
OFF_RV = OFF_RK + RET_WIDTH
OFF_RG = OFF_RV + RET_WIDTH
OFF_PU = OFF_RG + RET_WIDTH
OFF_GV = OFF_PU + POOL_WIDTH
OFF_GR = OFF_GV + GLA_WIDTH
OFF_GQ = OFF_GR + GLA_WIDTH
OFF_GK = OFF_GQ + GLA_KEY_PAD_WIDTH
OFF_GZ = OFF_GK + GLA_KEY_PAD_WIDTH
Z_WIDTH = OFF_GZ + LANE
KEYS_WIDTH = Z_WIDTH - OFF_GQ

MIX_RET = 0
MIX_POOL = RET_WIDTH
MIX_GLA = RET_WIDTH + POOL_WIDTH

GLA_WIN = 256
GLA_VWIN = (0, 128, 384, 512)

VMEM_LIMIT = 60 * 1024 * 1024

MIX_CHUNK = 128
MLP_TF = 1024


def _rms(x):
    return x * lax.rsqrt(jnp.mean(x * x, axis=-1, keepdims=True) + EPS)


def _silu(x):
    return x * jax.nn.sigmoid(x)


def _dot(a, b):
    return jnp.dot(a, b, preferred_element_type=F32)


def _dot_nt(a, b):
    return lax.dot_general(a, b, (((1,), (1,)), ((), ())), preferred_element_type=F32)


def _dot_tn(a, b):
    return lax.dot_general(a, b, (((0,), (0,)), ((), ())), preferred_element_type=F32)


def _ada_kernel(c_ref, wa_ref, wb_ref, b_ref, o_ref):
    k = pl.program_id(1)
    tk = wa_ref.shape[0]

    @pl.when(k == 0)
    def _():
        o_ref[...] = jnp.broadcast_to(b_ref[...], o_ref.shape)

    s = _silu(c_ref[...]).astype(BF16)
    o_ref[...] += (_dot(s[:, :tk], wa_ref[...].astype(BF16))
                   + _dot(s[:, tk:], wb_ref[...].astype(BF16)))


def _ada(c_pad, w_ada, b_ada, tk=128):
    depth, d, n = w_ada.shape
    rows = c_pad.shape[0]
    return pl.pallas_call(
        _ada_kernel,
        out_shape=jax.ShapeDtypeStruct((depth, rows, n), F32),
        grid=(depth, d // (2 * tk)),
        in_specs=[
            pl.BlockSpec((rows, 2 * tk), lambda l, k: (0, k)),
            pl.BlockSpec((None, tk, n), lambda l, k: (l, 2 * k, 0)),
            pl.BlockSpec((None, tk, n), lambda l, k: (l, 2 * k + 1, 0)),
            pl.BlockSpec((None, 1, n), lambda l, k: (l, 0, 0)),
        ],
        out_specs=pl.BlockSpec((None, rows, n), lambda l, k: (l, 0, 0)),
        compiler_params=pltpu.CompilerParams(
            dimension_semantics=("arbitrary", "arbitrary"),
            vmem_limit_bytes=VMEM_LIMIT),
        name="ada_mod",
    )(c_pad, w_ada, w_ada, b_ada.reshape(depth, 1, n))


def _inproj_kernel(x_ref, mod_ref, g_ref, w_ref, wk_ref, z_ref, *, col_chunk):
    x = x_ref[...]
    h = _rms(x) * g_ref[...]
    h = h * (1.0 + mod_ref[0, 1:2, :]) + mod_ref[0, 0:1, :]
    hb = h.astype(BF16)
    pieces = ((0, OFF_GV, w_ref, 0), (OFF_GV, OFF_GQ, w_ref, SRC_GV),
              (OFF_GQ, Z_WIDTH, wk_ref, 0))
    for z0, z1, ref, src0 in pieces:
        for c0 in range(z0, z1, col_chunk):
            c1 = min(c0 + col_chunk, z1)
            w = ref[:, src0 + c0 - z0:src0 + c1 - z0]
            z_ref[:, c0:c1] = _dot(hb, w).astype(BF16)


def _inproj(x2, mod_all, g_all, w_all, wk_all, layer, seq, tm=512, col_chunk=512):
    t, d = x2.shape
    per_batch = seq // tm
    resident = lambda a: pl.BlockSpec((None,) + a.shape[1:], lambda i: (layer, 0, 0),
                                      pipeline_mode=pl.Buffered(1))
    return pl.pallas_call(
        functools.partial(_inproj_kernel, col_chunk=col_chunk),
        out_shape=jax.ShapeDtypeStruct((t, Z_WIDTH), BF16),
        grid=(t // tm,),
        in_specs=[
            pl.BlockSpec((tm, d), lambda i: (i, 0)),
            pl.BlockSpec((None, 1, N_MOD, d), lambda i: (layer, i // per_batch, 0, 0)),
            pl.BlockSpec((None, 1, d), lambda i: (layer, 0, 0)),
            resident(w_all), resident(wk_all),
        ],
        out_specs=pl.BlockSpec((tm, Z_WIDTH), lambda i: (i, 0)),
        compiler_params=pltpu.CompilerParams(
            dimension_semantics=("arbitrary",),
            vmem_limit_bytes=VMEM_LIMIT),
        name="in_proj",
    )(x2, mod_all, g_all, w_all, wk_all)


def _retention_chunk(z_ref, rows, cos, sin, dec_ref, zeta_ref, xi_ref, gc_ref,
                     r_state, mix_ref):
    heads = range(RET_HEADS)
    cols = [slice(h * RET_DIM, (h + 1) * RET_DIM) for h in heads]
    qx, kb, scores, outs = [], [], [], []
    for h in heads:
        q = z_ref[rows, OFF_RQ + cols[h].start:OFF_RQ + cols[h].stop].astype(F32)
        k = z_ref[rows, OFF_RK + cols[h].start:OFF_RK + cols[h].stop].astype(F32)
        qr = q * cos + pltpu.roll(q, RET_DIM // 2, 1) * sin
        kr = k * cos + pltpu.roll(k, RET_DIM // 2, 1) * sin
        kb.append(kr.astype(BF16))
        qx.append((qr * xi_ref[h]).astype(BF16))
        scores.append(_dot_nt(qr.astype(BF16), kb[h]))
    for h in heads:
        v = z_ref[rows, OFF_RV + cols[h].start:OFF_RV + cols[h].stop]
        state = r_state[h]
        lhs = jnp.concatenate([(scores[h] * dec_ref[h]).astype(BF16), qx[h]], axis=1)
        rhs = jnp.concatenate([v, state.astype(BF16)], axis=0)
        outs.append(_dot(lhs, rhs))
        vz = (v.astype(F32) * zeta_ref[h]).astype(BF16)
        r_state[h] = gc_ref[h] * state + _dot_tn(kb[h], vz)
    for h in heads:
        g = z_ref[rows, OFF_RG + cols[h].start:OFF_RG + cols[h].stop].astype(F32)
        o = outs[h]
        mu = jnp.mean(o, axis=-1, keepdims=True)
        oc = o - mu
        var = jnp.mean(oc * oc, axis=-1, keepdims=True)
        out = oc * lax.rsqrt(var + EPS) * _silu(g)
        mix_ref[rows, MIX_RET + cols[h].start:MIX_RET + cols[h].stop] = out.astype(BF16)


def _pool_chunk(z_ref, rows, tok0, wp_ref, sp_ref, pext, mix_ref):
    n = MIX_CHUNK
    pu = z_ref[rows, OFF_PU:OFF_PU + POOL_WIDTH].astype(F32)
    pext[POOL_HALO:POOL_HALO + n, :] = pu
    tok = tok0 + lax.broadcasted_iota(jnp.int32, (n, 1), 0)
    for grp, win in enumerate(POOL_WINDOWS):
        c0 = grp * POOL_DIM
        cur = pu[:, c0:c0 + POOL_DIM]
        acc = cur
        for s in range(1, win):
            acc = acc + pext[POOL_HALO - s:POOL_HALO - s + n, c0:c0 + POOL_DIM]
        cnt = jnp.minimum(tok + 1, win).astype(F32)
        pooled = acc / cnt - cur
        y = _dot(pooled.astype(BF16), wp_ref[grp]) * sp_ref[:, c0:c0 + POOL_DIM]
        mix_ref[rows, MIX_POOL + c0:MIX_POOL + c0 + POOL_DIM] = y.astype(BF16)
    pext[0:POOL_HALO, :] = pext[n:n + POOL_HALO, :]


def _gla_gates(z_ref, rows, wg_ref, bg_ref, ltri_ref):
    n, half = MIX_CHUNK, GLA_CHUNK
    gate = _dot(z_ref[rows, OFF_GZ:OFF_GZ + LANE], wg_ref[...]) + bg_ref[...]
    log_a = jax.nn.log_sigmoid(gate) / GLA_TAU
    la_hi = log_a.astype(BF16)
    rem = log_a - la_hi.astype(F32)
    la_mid = rem.astype(BF16)
    la_lo = (rem - la_mid.astype(F32)).astype(BF16)
    ltri = ltri_ref[...]
    b = _dot(ltri, la_hi) + _dot(ltri, la_mid) + _dot(ltri, la_lo)

    second = lax.broadcasted_iota(jnp.int32, (n, 1), 0) >= half
    b_last0 = b[half - 1:half, :]
    b_last1 = b[n - 1:n, :]
    g0 = jnp.exp(b_last0)
    g1 = jnp.exp(b_last1)
    q = z_ref[rows, OFF_GQ:OFF_GQ + GLA_KEY_PAD_WIDTH].astype(F32) * (GLA_K_DIM ** -0.5)
    k = z_ref[rows, OFF_GK:OFF_GK + GLA_KEY_PAD_WIDTH].astype(F32)
    q_dec = q * jnp.exp(b)
    k_inv = k * jnp.exp(-b)
    k_state = k * jnp.exp(jnp.where(second, b_last1, b_last0) - b)
    q_cross = jnp.where(second, q_dec * g0, q_dec)
    k_upd = jnp.where(second, k_state, k_state * g1)
    k_first = jnp.where(second, 0.0, k_state)
    g_both = g0 * g1
    return q_dec, k_inv, q_cross, k_upd, k_first, g_both


def _gla_heads(z_ref, rows, gates, s_state, mix_ref):
    q_dec, k_inv, q_cross, k_upd, k_first, g_both = gates
    n, half = MIX_CHUNK, GLA_CHUNK
    second = lax.broadcasted_iota(jnp.int32, (n, 1), 0) >= half
    ri = lax.broadcasted_iota(jnp.int32, (n, n), 0)
    ci = lax.broadcasted_iota(jnp.int32, (n, n), 1)
    intra = ((ri >= half) == (ci >= half)) & (ri >= ci)
    lane = lax.broadcasted_iota(jnp.int32, (1, GLA_WIN), 1)

    heads = range(GLA_HEADS)
    kcs = [slice(h * GLA_K_PAD, (h + 1) * GLA_K_PAD) for h in heads]
    both, cross, vals, outs = [], [], [], []
    for h in heads:
        kc = kcs[h]
        keys = jnp.concatenate([k_inv[:, kc].astype(BF16), k_first[:, kc].astype(BF16)],
                               axis=0)
        both.append(_dot_nt(q_dec[:, kc].astype(BF16), keys))
        cross.append(_dot_nt(q_cross[:, kc].astype(BF16), s_state[h].astype(BF16)))
    for h in heads:
        kc = kcs[h]
        v0 = GLA_VWIN[h]
        v = z_ref[rows, OFF_GV + v0:OFF_GV + v0 + GLA_WIN]
        scores = (jnp.where(intra, both[h][:, :n], 0.0)
                  + jnp.where(second, both[h][:, n:], 0.0))
        vals.append(_dot(scores.astype(BF16), v) + cross[h])
        s_state[h] = s_state[h] * g_both[:, kc] + _dot_tn(v, k_upd[:, kc].astype(BF16))
    for h in heads:
        v0 = GLA_VWIN[h]
        vcol = lane + v0
        vmask = (vcol >= h * GLA_V_DIM) & (vcol < (h + 1) * GLA_V_DIM)
        gr = z_ref[rows, OFF_GR + v0:OFF_GR + v0 + GLA_WIN].astype(F32)
        o = vals[h]
        ms = jnp.sum(jnp.where(vmask, o * o, 0.0), axis=-1, keepdims=True)
        on = o * lax.rsqrt(ms / GLA_V_DIM + EPS)
        outs.append(jnp.where(vmask, on * _silu(gr), 0.0))
    tiles = [
        outs[0][:, :LANE],
        outs[0][:, LANE:] + outs[1][:, :LANE],
        outs[1][:, LANE:],
        outs[2][:, :LANE],
        outs[2][:, LANE:] + outs[3][:, :LANE],
        outs[3][:, LANE:],
    ]
    for t, tile in enumerate(tiles):
        mix_ref[rows, MIX_GLA + t * LANE:MIX_GLA + (t + 1) * LANE] = tile.astype(BF16)


def _mixer_kernel(z_ref, cos_ref, sin_ref, dec_ref, zeta_ref, xi_ref, gc_ref,
                  wg_ref, bg_ref, wp_ref, sp_ref, ltri_ref, mix_ref,
                  r_state, s_state, pext):
    sblk = pl.program_id(1)
    tb = z_ref.shape[0]

    @pl.when(sblk == 0)
    def _():
        r_state[...] = jnp.zeros_like(r_state)
        s_state[...] = jnp.zeros_like(s_state)
        pext[...] = jnp.zeros_like(pext)

    def chunk(c, carry):
        row0 = pl.multiple_of(c * MIX_CHUNK, MIX_CHUNK)
        rows = pl.ds(row0, MIX_CHUNK)
        cos = cos_ref[rows, :]
        sin = sin_ref[rows, :]
        gates = _gla_gates(z_ref, rows, wg_ref, bg_ref, ltri_ref)
        _retention_chunk(z_ref, rows, cos, sin, dec_ref, zeta_ref, xi_ref, gc_ref,
                         r_state, mix_ref)
        _gla_heads(z_ref, rows, gates, s_state, mix_ref)
        _pool_chunk(z_ref, rows, sblk * tb + row0, wp_ref, sp_ref, pext, mix_ref)
        return carry

    lax.fori_loop(0, tb // MIX_CHUNK, chunk, 0)


def _mixer(z, tables, wg, bg, wp, sp, layer, batch, seq, tb=512):
    t, zw = z.shape
    nblk = seq // tb
    cos2, sin2, dec, zeta, xi, gc, ltri = tables
    full = lambda shape: pl.BlockSpec(shape, lambda b, s: (0,) * len(shape))
    per_layer = lambda a: pl.BlockSpec(
        (None,) + a.shape[1:], lambda b, s: (layer,) + (0,) * (a.ndim - 1))
    return pl.pallas_call(
        _mixer_kernel,
        out_shape=jax.ShapeDtypeStruct((t, D_MODEL), BF16),
        grid=(batch, nblk),
        in_specs=[
            pl.BlockSpec((tb, zw), lambda b, s: (b * nblk + s, 0)),
            pl.BlockSpec((tb, RET_DIM), lambda b, s: (s, 0)),
            pl.BlockSpec((tb, RET_DIM), lambda b, s: (s, 0)),
            full(dec.shape), full(zeta.shape), full(xi.shape), full(gc.shape),
            per_layer(wg), per_layer(bg), per_layer(wp), per_layer(sp),
            full(ltri.shape),
        ],
        out_specs=pl.BlockSpec((tb, D_MODEL), lambda b, s: (b * nblk + s, 0)),
        scratch_shapes=[
            pltpu.VMEM((RET_HEADS, RET_DIM, RET_DIM), F32),
            pltpu.VMEM((GLA_HEADS, GLA_WIN, GLA_K_PAD), F32),
            pltpu.VMEM((POOL_HALO + MIX_CHUNK, POOL_WIDTH), F32),
        ],
        compiler_params=pltpu.CompilerParams(
            dimension_semantics=("arbitrary", "arbitrary"),
            vmem_limit_bytes=VMEM_LIMIT),
        name="token_mixer",
    )(z, cos2, sin2, dec, zeta, xi, gc, wg, bg, wp, sp, ltri)


def _outproj_kernel(mix_ref, x_ref, mod_ref, w_ref, o_ref):
    y = _dot(mix_ref[...], w_ref[...])
    o_ref[...] = x_ref[...] + mod_ref[0, 2:3, :] * y


def _outproj(mix, x2, mod_all, w_all, layer, seq, tm=512):
    t, d = x2.shape
    per_batch = seq // tm
    return pl.pallas_call(
        _outproj_kernel,
        out_shape=jax.ShapeDtypeStruct((t, d), F32),
        grid=(t // tm,),
        in_specs=[
            pl.BlockSpec((tm, d), lambda i: (i, 0)),
            pl.BlockSpec((tm, d), lambda i: (i, 0)),
            pl.BlockSpec((None, 1, N_MOD, d), lambda i: (layer, i // per_batch, 0, 0)),
            pl.BlockSpec((None, d, d), lambda i: (layer, 0, 0),
                         pipeline_mode=pl.Buffered(1)),
        ],
        out_specs=pl.BlockSpec((tm, d), lambda i: (i, 0)),
        compiler_params=pltpu.CompilerParams(
            dimension_semantics=("arbitrary",),
            vmem_limit_bytes=VMEM_LIMIT),
        name="out_proj",
    )(mix, x2, mod_all, w_all)


def _mlp_norm(x, mod_ref, g_ref):
    h = _rms(x) * g_ref[...]
    return (h * (1.0 + mod_ref[0, 4:5, :]) + mod_ref[0, 3:4, :]).astype(BF16)


def _mlp_kernel(x_ref, xn_ref, mod_ref, modn_ref, g_ref, gf_ref, wu0_ref, wu1_ref,
                wd0_ref, wd1_ref, o_ref, h_scr, *, final_norm):
    i = pl.program_id(0)
    f = pl.program_id(1)
    nf = pl.num_programs(1)
    tm = x_ref.shape[0]
    rows_per_step = tm // nf
    cur = i % 2

    @pl.when((i == 0) & (f == 0))
    def _():
        h_scr[0] = _mlp_norm(x_ref[...], mod_ref, g_ref)

    def step(first, last):
        dh = wu0_ref.shape[0]
        a = (_dot(h_scr[cur, :, :dh], wu0_ref[...])
             + _dot(h_scr[cur, :, dh:], wu1_ref[...]))
        r = jnp.square(jnp.maximum(a, 0.0)).astype(BF16)
        th = wd0_ref.shape[0]
        y = _dot(r[:, :th], wd0_ref[...]) + _dot(r[:, th:], wd1_ref[...])
        if first:
            o_ref[...] = y
        elif not last:
            o_ref[...] += y
        else:
            xo = x_ref[...] + mod_ref[0, 5:6, :] * (o_ref[...] + y)
            if final_norm:
                xo = _rms(xo) * gf_ref[...]
            o_ref[...] = xo
        r0 = pl.multiple_of(f * rows_per_step, rows_per_step)
        h_scr[1 - cur, pl.ds(r0, rows_per_step), :] = _mlp_norm(
            xn_ref[pl.ds(r0, rows_per_step), :], modn_ref, g_ref)

    pl.when(f == 0)(functools.partial(step, True, False))
    pl.when((f > 0) & (f < nf - 1))(functools.partial(step, False, False))
    pl.when(f == nf - 1)(functools.partial(step, False, True))


def _mlp(x2, mod_all, g_all, gf, wu_all, wd_all, layer, seq, final_norm, tm=512,
         tf=MLP_TF):
    t, d = x2.shape
    nf = wu_all.shape[2] // tf
    assert nf >= 2, "the first and last hidden tiles take different code paths"
    per_batch = seq // tm
    last_blk = t // tm - 1
    nxt = lambda i: jnp.minimum(i + 1, last_blk)
    return pl.pallas_call(
        functools.partial(_mlp_kernel, final_norm=final_norm),
        out_shape=jax.ShapeDtypeStruct((t, d), F32),
        grid=(t // tm, nf),
        in_specs=[
            pl.BlockSpec((tm, d), lambda i, f: (i, 0)),
            pl.BlockSpec((tm, d), lambda i, f: (nxt(i), 0)),
            pl.BlockSpec((None, 1, N_MOD, d), lambda i, f: (layer, i // per_batch, 0, 0)),
            pl.BlockSpec((None, 1, N_MOD, d),
                         lambda i, f: (layer, nxt(i) // per_batch, 0, 0)),
            pl.BlockSpec((None, 1, d), lambda i, f: (layer, 0, 0)),
            pl.BlockSpec((1, d), lambda i, f: (0, 0)),
            pl.BlockSpec((None, d // 2, tf), lambda i, f: (layer, 0, f)),
            pl.BlockSpec((None, d // 2, tf), lambda i, f: (layer, 1, f)),
            pl.BlockSpec((None, tf // 2, d), lambda i, f: (layer, 2 * f, 0)),
            pl.BlockSpec((None, tf // 2, d), lambda i, f: (layer, 2 * f + 1, 0)),
        ],
        out_specs=pl.BlockSpec((tm, d), lambda i, f: (i, 0)),
        scratch_shapes=[pltpu.VMEM((2, tm, d), BF16)],
        compiler_params=pltpu.CompilerParams(
            dimension_semantics=("arbitrary", "arbitrary"),
            vmem_limit_bytes=VMEM_LIMIT),
        name="mlp",
    )(x2, x2, mod_all, mod_all, g_all, gf, wu_all, wu_all, wd_all, wd_all)


def _tables(seq):
    pos = jnp.arange(seq, dtype=F32)
    inv_freq = ROPE_BASE ** (-jnp.arange(0, RET_DIM, 2, dtype=F32) / RET_DIM)
    ang = pos[:, None] * inv_freq[None, :]
    cos, sin = jnp.cos(ang), jnp.sin(ang)
    cos2 = jnp.concatenate([cos, cos], axis=-1)
    sin2 = jnp.concatenate([-sin, sin], axis=-1)

    c = RET_CHUNK
    log_g = jnp.log1p(-jnp.exp2(-5.0 - jnp.arange(RET_HEADS, dtype=F32)))
    idx = jnp.arange(c, dtype=F32)
    rel = idx[:, None] - idx[None, :]
    decay = jnp.where(rel[None] >= 0,
                      jnp.exp(jnp.maximum(rel, 0.0)[None] * log_g[:, None, None]), 0.0)
    scale = RET_DIM ** -0.5
    dec = decay * scale
    zeta = jnp.exp((c - 1.0 - idx)[None, :] * log_g[:, None])
    zeta = jnp.broadcast_to(zeta[:, :, None], (RET_HEADS, c, RET_DIM))
    xi = jnp.exp((idx + 1.0)[None, :] * log_g[:, None]) * scale
    xi = jnp.broadcast_to(xi[:, :, None], (RET_HEADS, c, RET_DIM))
    gc = jnp.broadcast_to(jnp.exp(c * log_g)[:, None, None], (RET_HEADS, 1, RET_DIM))

    r = jnp.arange(MIX_CHUNK)
    same = (r[:, None] // GLA_CHUNK) == (r[None, :] // GLA_CHUNK)
    ltri = jnp.where(same & (r[:, None] >= r[None, :]), 1.0, 0.0).astype(BF16)
    return cos2, sin2, dec, zeta, xi, gc, ltri


def _pad_key_heads(a):
    lead = a.shape[:-1]
    a = a.reshape(lead + (GLA_HEADS, GLA_K_DIM))
    a = jnp.pad(a, [(0, 0)] * (len(lead) + 1) + [(0, GLA_K_PAD - GLA_K_DIM)])
    return a.reshape(lead + (GLA_KEY_PAD_WIDTH,))


def _key_columns(a):
    pad_rank = [(0, 0)] * (a.ndim - 1) + [(0, LANE - GLA_RANK)]
    return jnp.concatenate([
        _pad_key_heads(a[..., SRC_GQ:SRC_GK]),
        _pad_key_heads(a[..., SRC_GK:SRC_GV]),
        jnp.pad(a[..., SRC_GZ:], pad_rank),
    ], axis=-1)


def kernel(x, c, w_ada, b_ada, g_mix, g_mlp, w_in, w_gate_up, b_gate, w_pool, s_pool,
           w_out, w_up, w_down, g_final):
    batch, seq, d = x.shape
    depth = w_ada.shape[0]
    tables = _tables(seq)

    c_pad = jnp.pad(c, ((0, 8 - batch), (0, 0)))
    mod_all = _ada(c_pad, w_ada, b_ada)[:, :batch].reshape(depth, batch, N_MOD, d)

    w_in_b = w_in.astype(BF16)
    w_keys_b = _key_columns(w_in).astype(BF16)
    w_out_b = w_out.astype(BF16)
    w_up_b = w_up.astype(BF16)
    w_down_b = w_down.astype(BF16)
    wg_b = jnp.pad(_pad_key_heads(w_gate_up),
                   ((0, 0), (0, LANE - GLA_RANK), (0, 0))).astype(BF16)
    wp_b = w_pool.astype(BF16)
    gf = g_final.reshape(1, d)
    g_mix3 = g_mix.reshape(depth, 1, d)
    g_mlp3 = g_mlp.reshape(depth, 1, d)
    bg3 = _pad_key_heads(b_gate).reshape(depth, 1, -1)
    sp3 = s_pool.reshape(depth, 1, -1)

    x2 = x.reshape(batch * seq, d)
    for l in range(depth):
        z = _inproj(x2, mod_all, g_mix3, w_in_b, w_keys_b, l, seq)
        mix = _mixer(z, tables, wg_b, bg3, wp_b, sp3, l, batch, seq)
        x2 = _outproj(mix, x2, mod_all, w_out_b, l, seq)
        x2 = _mlp(x2, mod_all, g_mlp3, gf, w_up_b, w_down_b, l, seq,
                  final_norm=(l == depth - 1))
    return x2.reshape(batch, seq, d)
```

```python
import functools

import jax
import jax.numpy as jnp
from jax import lax
from jax.experimental import pallas as pl
from jax.experimental.pallas import tpu as pltpu

F32 = jnp.float32
BF16 = jnp.bfloat16

D_MODEL = 2048
DEPTH = 4
N_MOD = 6
EPS = 1e-6
D_FF = 4 * D_MODEL

RET_HEADS = 6
RET_DIM = 128
RET_WIDTH = RET_HEADS * RET_DIM
RET_CHUNK = 128
ROPE_BASE = 10000.0
POOL_WINDOWS = (2, 4, 8, 16)
POOL_GROUPS = 4
POOL_DIM = 128
POOL_WIDTH = POOL_GROUPS * POOL_DIM
POOL_HALO = 16
GLA_HEADS = 4
GLA_V_DIM = 192
GLA_K_DIM = 96
GLA_WIDTH = GLA_HEADS * GLA_V_DIM
GLA_KEY_WIDTH = GLA_HEADS * GLA_K_DIM
GLA_RANK = 16
GLA_TAU = 16.0
GLA_CHUNK = 64

LANE = 128

SRC_GQ = 4 * RET_WIDTH + POOL_WIDTH
SRC_GK = SRC_GQ + GLA_KEY_WIDTH
SRC_GV = SRC_GK + GLA_KEY_WIDTH
SRC_GZ = SRC_GV + 2 * GLA_WIDTH
IN_WIDTH = SRC_GZ + GLA_RANK

GLA_K_PAD = LANE
GLA_KEY_PAD_WIDTH = GLA_HEADS * GLA_K_PAD
OFF_RQ = 0
OFF_RK = OFF_RQ + RET_WIDTH
OFF_RV = OFF_RK + RET_WIDTH
OFF_RG = OFF_RV + RET_WIDTH
OFF_PU = OFF_RG + RET_WIDTH
OFF_GV = OFF_PU + POOL_WIDTH
OFF_GR = OFF_GV + GLA_WIDTH
OFF_GQ = OFF_GR + GLA_WIDTH
OFF_GK = OFF_GQ + GLA_KEY_PAD_WIDTH
OFF_GZ = OFF_GK + GLA_KEY_PAD_WIDTH
Z_WIDTH = OFF_GZ + LANE
KEYS_WIDTH = Z_WIDTH - OFF_GQ

MIX_RET = 0
MIX_POOL = RET_WIDTH
MIX_GLA = RET_WIDTH + POOL_WIDTH

GLA_WIN = 256
GLA_VWIN = (0, 128, 384, 512)

VMEM_LIMIT = 60 * 1024 * 1024

MIX_CHUNK = 128
MLP_TF = 1024


def _rms(x):
    return x * lax.rsqrt(jnp.mean(x * x, axis=-1, keepdims=True) + EPS)


def _silu(x):
    return x * jax.nn.sigmoid(x)


def _dot(a, b):
    return jnp.dot(a, b, preferred_element_type=F32)


def _dot_nt(a, b):
    return lax.dot_general(a, b, (((1,), (1,)), ((), ())), preferred_element_type=F32)


def _dot_tn(a, b):
    return lax.dot_general(a, b, (((0,), (0,)), ((), ())), preferred_element_type=F32)


def _ada_kernel(c_ref, wa_ref, wb_ref, b_ref, o_ref):
    k = pl.program_id(1)
    tk = wa_ref.shape[0]

    @pl.when(k == 0)
    def _():
        o_ref[...] = jnp.broadcast_to(b_ref[...], o_ref.shape)

    s = _silu(c_ref[...]).astype(BF16)
    o_ref[...] += (_dot(s[:, :tk], wa_ref[...].astype(BF16))
                   + _dot(s[:, tk:], wb_ref[...].astype(BF16)))


def _ada(c_pad, w_ada, b_ada, tk=128):
    depth, d, n = w_ada.shape
    rows = c_pad.shape[0]
    return pl.pallas_call(
        _ada_kernel,
        out_shape=jax.ShapeDtypeStruct((depth, rows, n), F32),
        grid=(depth, d // (2 * tk)),
        in_specs=[
            pl.BlockSpec((rows, 2 * tk), lambda l, k: (0, k)),
            pl.BlockSpec((None, tk, n), lambda l, k: (l, 2 * k, 0)),
            pl.BlockSpec((None, tk, n), lambda l, k: (l, 2 * k + 1, 0)),
            pl.BlockSpec((None, 1, n), lambda l, k: (l, 0, 0)),
        ],
        out_specs=pl.BlockSpec((None, rows, n), lambda l, k: (l, 0, 0)),
        compiler_params=pltpu.CompilerParams(
            dimension_semantics=("arbitrary", "arbitrary"),
            vmem_limit_bytes=VMEM_LIMIT),
        name="ada_mod",
    )(c_pad, w_ada, w_ada, b_ada.reshape(depth, 1, n))


def _inproj_stages(x_ref, mod_ref, g_ref, w_ref, wk_ref, z_ref, col_chunk):
    hb = []

    def norm():
        h = _rms(x_ref[...]) * g_ref[...]
        h = h * (1.0 + mod_ref[0, 1:2, :]) + mod_ref[0, 0:1, :]
        hb.append(h.astype(BF16))

    def chunk(ref, src, c0, c1):
        def run():
            z_ref[:, c0:c1] = _dot(hb[0], ref[:, src:src + c1 - c0]).astype(BF16)
        return run

    pieces = ((0, OFF_GV, w_ref, 0), (OFF_GV, OFF_GQ, w_ref, SRC_GV),
              (OFF_GQ, Z_WIDTH, wk_ref, 0))
    stages = [norm]
    for z0, z1, ref, src0 in pieces:
        for c0 in range(z0, z1, col_chunk):
            c1 = min(c0 + col_chunk, z1)
            stages.append(chunk(ref, src0 + c0 - z0, c0, c1))
    return stages


def _retention_stages(z_ref, rows, cos_ref, sin_ref, dec_ref, zeta_ref, xi_ref, gc_ref,
                      r_state, mix_ref):
    heads = range(RET_HEADS)
    cols = [slice(h * RET_DIM, (h + 1) * RET_DIM) for h in heads]
    qx, kb, scores, outs = [], [], [], []

    def rotate_and_score():
        cos = cos_ref[rows, :]
        sin = sin_ref[rows, :]
        for h in heads:
            q = z_ref[rows, OFF_RQ + cols[h].start:OFF_RQ + cols[h].stop].astype(F32)
            k = z_ref[rows, OFF_RK + cols[h].start:OFF_RK + cols[h].stop].astype(F32)
            qr = q * cos + pltpu.roll(q, RET_DIM // 2, 1) * sin
            kr = k * cos + pltpu.roll(k, RET_DIM // 2, 1) * sin
            kb.append(kr.astype(BF16))
            qx.append((qr * xi_ref[h]).astype(BF16))
            scores.append(_dot_nt(qr.astype(BF16), kb[h]))

    def attend_and_update():
        for h in heads:
            v = z_ref[rows, OFF_RV + cols[h].start:OFF_RV + cols[h].stop]
            state = r_state[h]
            lhs = jnp.concatenate([(scores[h] * dec_ref[h]).astype(BF16), qx[h]], axis=1)
            rhs = jnp.concatenate([v, state.astype(BF16)], axis=0)
            outs.append(_dot(lhs, rhs))
            vz = (v.astype(F32) * zeta_ref[h]).astype(BF16)
            r_state[h] = gc_ref[h] * state + _dot_tn(kb[h], vz)

    def norm_and_gate():
        for h in heads:
            g = z_ref[rows, OFF_RG + cols[h].start:OFF_RG + cols[h].stop].astype(F32)
            o = outs[h]
            mu = jnp.mean(o, axis=-1, keepdims=True)
            oc = o - mu
            var = jnp.mean(oc * oc, axis=-1, keepdims=True)
            out = oc * lax.rsqrt(var + EPS) * _silu(g)
            mix_ref[rows, MIX_RET + cols[h].start:MIX_RET + cols[h].stop] = out.astype(BF16)

    return [rotate_and_score, attend_and_update, norm_and_gate]


def _pool_chunk(z_ref, rows, tok0, wp_ref, sp_ref, pext, mix_ref):
    n = MIX_CHUNK
    pu = z_ref[rows, OFF_PU:OFF_PU + POOL_WIDTH].astype(F32)
    pext[POOL_HALO:POOL_HALO + n, :] = pu
    tok = tok0 + lax.broadcasted_iota(jnp.int32, (n, 1), 0)
    for grp, win in enumerate(POOL_WINDOWS):
        c0 = grp * POOL_DIM
        cur = pu[:, c0:c0 + POOL_DIM]
        acc = cur
        for s in range(1, win):
            acc = acc + pext[POOL_HALO - s:POOL_HALO - s + n, c0:c0 + POOL_DIM]
        cnt = jnp.minimum(tok + 1, win).astype(F32)
        pooled = acc / cnt - cur
        y = _dot(pooled.astype(BF16), wp_ref[grp]) * sp_ref[:, c0:c0 + POOL_DIM]
        mix_ref[rows, MIX_POOL + c0:MIX_POOL + c0 + POOL_DIM] = y.astype(BF16)
    pext[0:POOL_HALO, :] = pext[n:n + POOL_HALO, :]


def _gla_gates(z_ref, rows, wg_ref, bg_ref, ltri_ref):
    n, half = MIX_CHUNK, GLA_CHUNK
    gate = _dot(z_ref[rows, OFF_GZ:OFF_GZ + LANE], wg_ref[...]) + bg_ref[...]
    log_a = jax.nn.log_sigmoid(gate) / GLA_TAU
    la_hi = log_a.astype(BF16)
    rem = log_a - la_hi.astype(F32)
    la_mid = rem.astype(BF16)
    la_lo = (rem - la_mid.astype(F32)).astype(BF16)
    ltri = ltri_ref[...]
    b = _dot(ltri, la_hi) + _dot(ltri, la_mid) + _dot(ltri, la_lo)

    second = lax.broadcasted_iota(jnp.int32, (n, 1), 0) >= half
    b_last0 = b[half - 1:half, :]
    b_last1 = b[n - 1:n, :]
    g0 = jnp.exp(b_last0)
    g1 = jnp.exp(b_last1)
    q = z_ref[rows, OFF_GQ:OFF_GQ + GLA_KEY_PAD_WIDTH].astype(F32) * (GLA_K_DIM ** -0.5)
    k = z_ref[rows, OFF_GK:OFF_GK + GLA_KEY_PAD_WIDTH].astype(F32)
    q_dec = q * jnp.exp(b)
    k_inv = k * jnp.exp(-b)
    k_state = k * jnp.exp(jnp.where(second, b_last1, b_last0) - b)
    q_cross = jnp.where(second, q_dec * g0, q_dec)
    k_upd = jnp.where(second, k_state, k_state * g1)
    k_first = jnp.where(second, 0.0, k_state)
    g_both = g0 * g1
    return q_dec, k_inv, q_cross, k_upd, k_first, g_both


def _gla_stages(z_ref, rows, wg_ref, bg_ref, ltri_ref, s_state, mix_ref):
    n, half = MIX_CHUNK, GLA_CHUNK
    heads = range(GLA_HEADS)
    kcs = [slice(h * GLA_K_PAD, (h + 1) * GLA_K_PAD) for h in heads]
    gates, both, cross, vals = [], [], [], []

    def gate_chain():
        gates.extend(_gla_gates(z_ref, rows, wg_ref, bg_ref, ltri_ref))

    def score():
        q_dec, k_inv, q_cross, _, k_first, _ = gates
        for h in heads:
            kc = kcs[h]
            keys = jnp.concatenate(
                [k_inv[:, kc].astype(BF16), k_first[:, kc].astype(BF16)], axis=0)
            both.append(_dot_nt(q_dec[:, kc].astype(BF16), keys))
            cross.append(_dot_nt(q_cross[:, kc].astype(BF16), s_state[h].astype(BF16)))

    def attend_and_update():
        k_upd, g_both = gates[3], gates[5]
        second = lax.broadcasted_iota(jnp.int32, (n, 1), 0) >= half
        ri = lax.broadcasted_iota(jnp.int32, (n, n), 0)
        ci = lax.broadcasted_iota(jnp.int32, (n, n), 1)
        intra = ((ri >= half) == (ci >= half)) & (ri >= ci)
        for h in heads:
            kc = kcs[h]
            v0 = GLA_VWIN[h]
            v = z_ref[rows, OFF_GV + v0:OFF_GV + v0 + GLA_WIN]
            scores = (jnp.where(intra, both[h][:, :n], 0.0)
                      + jnp.where(second, both[h][:, n:], 0.0))
            vals.append(_dot(scores.astype(BF16), v) + cross[h])
            s_state[h] = s_state[h] * g_both[:, kc] + _dot_tn(v, k_upd[:, kc].astype(BF16))

    def norm_and_gate():
        lane = lax.broadcasted_iota(jnp.int32, (1, GLA_WIN), 1)
        outs = []
        for h in heads:
            v0 = GLA_VWIN[h]
            vcol = lane + v0
            vmask = (vcol >= h * GLA_V_DIM) & (vcol < (h + 1) * GLA_V_DIM)
            gr = z_ref[rows, OFF_GR + v0:OFF_GR + v0 + GLA_WIN].astype(F32)
            o = vals[h]
            ms = jnp.sum(jnp.where(vmask, o * o, 0.0), axis=-1, keepdims=True)
            on = o * lax.rsqrt(ms / GLA_V_DIM + EPS)
            outs.append(jnp.where(vmask, on * _silu(gr), 0.0))
        tiles = [
            outs[0][:, :LANE],
            outs[0][:, LANE:] + outs[1][:, :LANE],
            outs[1][:, LANE:],
            outs[2][:, :LANE],
            outs[2][:, LANE:] + outs[3][:, :LANE],
            outs[3][:, LANE:],
        ]
        for t, tile in enumerate(tiles):
            mix_ref[rows, MIX_GLA + t * LANE:MIX_GLA + (t + 1) * LANE] = tile.astype(BF16)

    return [gate_chain, score, attend_and_update, norm_and_gate]


def _interleave(major, minor):
    order, placed = [], 0
    for i, stage in enumerate(major):
        order.append(stage)
        due = (i + 1) * len(minor) // len(major)
        order.extend(minor[placed:due])
        placed = due
    return order


def _inproj_mixer_step(x_ref, mod_ref, g_ref, w_ref, wk_ref, z_write, z_read, cos_ref,
                       sin_ref, dec_ref, zeta_ref, xi_ref, gc_ref, wg_ref, bg_ref, wp_ref,
                       sp_ref, ltri_ref, mix_ref, r_state, s_state, pext, tok_base,
                       col_chunk):
    project = _inproj_stages(x_ref, mod_ref, g_ref, w_ref, wk_ref, z_write, col_chunk)
    mix = []
    for c in range(z_read.shape[0] // MIX_CHUNK):
        rows = pl.ds(c * MIX_CHUNK, MIX_CHUNK)
        gate_chain, g_score, g_attend, g_out = _gla_stages(
            z_read, rows, wg_ref, bg_ref, ltri_ref, s_state, mix_ref)
        r_score, r_attend, r_out = _retention_stages(
            z_read, rows, cos_ref, sin_ref, dec_ref, zeta_ref, xi_ref, gc_ref, r_state,
            mix_ref)
        pool = functools.partial(_pool_chunk, z_read, rows, tok_base + c * MIX_CHUNK,
                                 wp_ref, sp_ref, pext, mix_ref)
        mix += [gate_chain, r_score, r_attend, g_score, r_out, g_attend, g_out, pool]
    for stage in _interleave(project, mix):
        stage()


def _inproj_mixer_kernel(x_ref, mod_ref, g_ref, w_ref, wk_ref, cos_ref, sin_ref, dec_ref,
                         zeta_ref, xi_ref, gc_ref, wg_ref, bg_ref, wp_ref, sp_ref,
                         ltri_ref, mix_ref, z_even, z_odd, r_state, s_state, pext, *,
                         blocks_per_seq, col_chunk):
    j = pl.program_id(0)
    tb = x_ref.shape[0]
    sblk = lax.rem(jnp.maximum(j - 1, 0), blocks_per_seq)

    @pl.when(j == 0)
    def _():
        z_odd[...] = jnp.zeros_like(z_odd)

    @pl.when(sblk == 0)
    def _():
        r_state[...] = jnp.zeros_like(r_state)
        s_state[...] = jnp.zeros_like(s_state)
        pext[...] = jnp.zeros_like(pext)

    step = functools.partial(
        _inproj_mixer_step, cos_ref=cos_ref, sin_ref=sin_ref, dec_ref=dec_ref,
        zeta_ref=zeta_ref, xi_ref=xi_ref, gc_ref=gc_ref, wg_ref=wg_ref, bg_ref=bg_ref,
        wp_ref=wp_ref, sp_ref=sp_ref, ltri_ref=ltri_ref, mix_ref=mix_ref,
        r_state=r_state, s_state=s_state, pext=pext, tok_base=sblk * tb,
        col_chunk=col_chunk)

    @pl.when(j % 2 == 0)
    def _():
        step(x_ref, mod_ref, g_ref, w_ref, wk_ref, z_even, z_odd)

    @pl.when(j % 2 == 1)
    def _():
        step(x_ref, mod_ref, g_ref, w_ref, wk_ref, z_odd, z_even)


def _inproj_mixer(x2, mod_all, g_all, w_all, wk_all, tables, wg, bg, wp, sp, layer, seq,
                  tb=256, col_chunk=512):
    t, d = x2.shape
    nblk = t // tb
    per_seq = seq // tb
    cos2, sin2, dec, zeta, xi, gc, ltri = tables
    cur = lambda j: jnp.minimum(j, nblk - 1)
    prev = lambda j: jnp.maximum(j - 1, 0)
    const = lambda a: pl.BlockSpec(a.shape, lambda j: (0,) * a.ndim,
                                   pipeline_mode=pl.Buffered(1))
    per_layer = lambda a: pl.BlockSpec(
        (None,) + a.shape[1:], lambda j: (layer,) + (0,) * (a.ndim - 1),
        pipeline_mode=pl.Buffered(1))
    return pl.pallas_call(
        functools.partial(_inproj_mixer_kernel, blocks_per_seq=per_seq,
                          col_chunk=col_chunk),
        out_shape=jax.ShapeDtypeStruct((t, D_MODEL), BF16),
        grid=(nblk + 1,),
        in_specs=[
            pl.BlockSpec((tb, d), lambda j: (cur(j), 0)),
            pl.BlockSpec((None, 1, N_MOD, d), lambda j: (layer, cur(j) // per_seq, 0, 0)),
            pl.BlockSpec((None, 1, d), lambda j: (layer, 0, 0)),
            per_layer(w_all), per_layer(wk_all),
            pl.BlockSpec((tb, RET_DIM), lambda j: (prev(j) % per_seq, 0)),
            pl.BlockSpec((tb, RET_DIM), lambda j: (prev(j) % per_seq, 0)),
            const(dec), const(zeta), const(xi), const(gc),
            per_layer(wg), per_layer(bg), per_layer(wp), per_layer(sp),
            const(ltri),
        ],
        out_specs=pl.BlockSpec((tb, D_MODEL), lambda j: (prev(j), 0)),
        scratch_shapes=[
            pltpu.VMEM((tb, Z_WIDTH), BF16),
            pltpu.VMEM((tb, Z_WIDTH), BF16),
            pltpu.VMEM((RET_HEADS, RET_DIM, RET_DIM), F32),
            pltpu.VMEM((GLA_HEADS, GLA_WIN, GLA_K_PAD), F32),
            pltpu.VMEM((POOL_HALO + MIX_CHUNK, POOL_WIDTH), F32),
        ],
        compiler_params=pltpu.CompilerParams(
            dimension_semantics=("arbitrary",),
            vmem_limit_bytes=VMEM_LIMIT),
        name="inproj_mixer",
    )(x2, mod_all, g_all, w_all, wk_all, cos2, sin2, dec, zeta, xi, gc, wg, bg, wp, sp,
      ltri)


def _outproj_kernel(mix_ref, x_ref, mod_ref, w_ref, o_ref):
    y = _dot(mix_ref[...], w_ref[...])
    o_ref[...] = x_ref[...] + mod_ref[0, 2:3, :] * y


def _outproj(mix, x2, mod_all, w_all, layer, seq, tm=512):
    t, d = x2.shape
    per_batch = seq // tm
    return pl.pallas_call(
        _outproj_kernel,
        out_shape=jax.ShapeDtypeStruct((t, d), F32),
        grid=(t // tm,),
        in_specs=[
            pl.BlockSpec((tm, d), lambda i: (i, 0)),
            pl.BlockSpec((tm, d), lambda i: (i, 0)),
            pl.BlockSpec((None, 1, N_MOD, d), lambda i: (layer, i // per_batch, 0, 0)),
            pl.BlockSpec((None, d, d), lambda i: (layer, 0, 0),
                         pipeline_mode=pl.Buffered(1)),
        ],
        out_specs=pl.BlockSpec((tm, d), lambda i: (i, 0)),
        compiler_params=pltpu.CompilerParams(
            dimension_semantics=("arbitrary",),
            vmem_limit_bytes=VMEM_LIMIT),
        name="out_proj",
    )(mix, x2, mod_all, w_all)


def _mlp_norm(x, mod_ref, g_ref):
    h = _rms(x) * g_ref[...]
    return (h * (1.0 + mod_ref[0, 4:5, :]) + mod_ref[0, 3:4, :]).astype(BF16)


def _mlp_kernel(x_ref, xn_ref, mod_ref, modn_ref, g_ref, gf_ref, wu0_ref, wu1_ref,
                wd0_ref, wd1_ref, o_ref, h_scr, *, final_norm):
    i = pl.program_id(0)
    f = pl.program_id(1)
    nf = pl.num_programs(1)
    tm = x_ref.shape[0]
    rows_per_step = tm // nf
    cur = i % 2

    @pl.when((i == 0) & (f == 0))
    def _():
        h_scr[0] = _mlp_norm(x_ref[...], mod_ref, g_ref)

    def step(first, last):
        dh = wu0_ref.shape[0]
        a = (_dot(h_scr[cur, :, :dh], wu0_ref[...])
             + _dot(h_scr[cur, :, dh:], wu1_ref[...]))
        r = jnp.square(jnp.maximum(a, 0.0)).astype(BF16)
        th = wd0_ref.shape[0]
        y = _dot(r[:, :th], wd0_ref[...]) + _dot(r[:, th:], wd1_ref[...])
        if first:
            o_ref[...] = y
        elif not last:
            o_ref[...] += y
        else:
            xo = x_ref[...] + mod_ref[0, 5:6, :] * (o_ref[...] + y)
            if final_norm:
                xo = _rms(xo) * gf_ref[...]
            o_ref[...] = xo
        r0 = pl.multiple_of(f * rows_per_step, rows_per_step)
        h_scr[1 - cur, pl.ds(r0, rows_per_step), :] = _mlp_norm(
            xn_ref[pl.ds(r0, rows_per_step), :], modn_ref, g_ref)

    pl.when(f == 0)(functools.partial(step, True, False))
    pl.when((f > 0) & (f < nf - 1))(functools.partial(step, False, False))
    pl.when(f == nf - 1)(functools.partial(step, False, True))


def _mlp(x2, mod_all, g_all, gf, wu_all, wd_all, layer, seq, final_norm, tm=512,
         tf=MLP_TF):
    t, d = x2.shape
    nf = wu_all.shape[2] // tf
    assert nf >= 2, "the first and last hidden tiles take different code paths"
    per_batch = seq // tm
    last_blk = t // tm - 1
    nxt = lambda i: jnp.minimum(i + 1, last_blk)
    return pl.pallas_call(
        functools.partial(_mlp_kernel, final_norm=final_norm),
        out_shape=jax.ShapeDtypeStruct((t, d), F32),
        grid=(t // tm, nf),
        in_specs=[
            pl.BlockSpec((tm, d), lambda i, f: (i, 0)),
            pl.BlockSpec((tm, d), lambda i, f: (nxt(i), 0)),
            pl.BlockSpec((None, 1, N_MOD, d), lambda i, f: (layer, i // per_batch, 0, 0)),
            pl.BlockSpec((None, 1, N_MOD, d),
                         lambda i, f: (layer, nxt(i) // per_batch, 0, 0)),
            pl.BlockSpec((None, 1, d), lambda i, f: (layer, 0, 0)),
            pl.BlockSpec((1, d), lambda i, f: (0, 0)),
            pl.BlockSpec((None, d // 2, tf), lambda i, f: (layer, 0, f)),
            pl.BlockSpec((None, d // 2, tf), lambda i, f: (layer, 1, f)),
            pl.BlockSpec((None, tf // 2, d), lambda i, f: (layer, 2 * f, 0)),
            pl.BlockSpec((None, tf // 2, d), lambda i, f: (layer, 2 * f + 1, 0)),
        ],
        out_specs=pl.BlockSpec((tm, d), lambda i, f: (i, 0)),
        scratch_shapes=[pltpu.VMEM((2, tm, d), BF16)],
        compiler_params=pltpu.CompilerParams(
            dimension_semantics=("arbitrary", "arbitrary"),
            vmem_limit_bytes=VMEM_LIMIT),
        name="mlp",
    )(x2, x2, mod_all, mod_all, g_all, gf, wu_all, wu_all, wd_all, wd_all)


def _tables(seq):
    pos = jnp.arange(seq, dtype=F32)
    inv_freq = ROPE_BASE ** (-jnp.arange(0, RET_DIM, 2, dtype=F32) / RET_DIM)
    ang = pos[:, None] * inv_freq[None, :]
    cos, sin = jnp.cos(ang), jnp.sin(ang)
    cos2 = jnp.concatenate([cos, cos], axis=-1)
    sin2 = jnp.concatenate([-sin, sin], axis=-1)

    c = RET_CHUNK
    log_g = jnp.log1p(-jnp.exp2(-5.0 - jnp.arange(RET_HEADS, dtype=F32)))
    idx = jnp.arange(c, dtype=F32)
    rel = idx[:, None] - idx[None, :]
    decay = jnp.where(rel[None] >= 0,
                      jnp.exp(jnp.maximum(rel, 0.0)[None] * log_g[:, None, None]), 0.0)
    scale = RET_DIM ** -0.5
    dec = decay * scale
    zeta = jnp.exp((c - 1.0 - idx)[None, :] * log_g[:, None])
    zeta = jnp.broadcast_to(zeta[:, :, None], (RET_HEADS, c, RET_DIM))
    xi = jnp.exp((idx + 1.0)[None, :] * log_g[:, None]) * scale
    xi = jnp.broadcast_to(xi[:, :, None], (RET_HEADS, c, RET_DIM))
    gc = jnp.broadcast_to(jnp.exp(c * log_g)[:, None, None], (RET_HEADS, 1, RET_DIM))

    r = jnp.arange(MIX_CHUNK)
    same = (r[:, None] // GLA_CHUNK) == (r[None, :] // GLA_CHUNK)
    ltri = jnp.where(same & (r[:, None] >= r[None, :]), 1.0, 0.0).astype(BF16)
    return cos2, sin2, dec, zeta, xi, gc, ltri


def _pad_key_heads(a):
    lead = a.shape[:-1]
    a = a.reshape(lead + (GLA_HEADS, GLA_K_DIM))
    a = jnp.pad(a, [(0, 0)] * (len(lead) + 1) + [(0, GLA_K_PAD - GLA_K_DIM)])
    return a.reshape(lead + (GLA_KEY_PAD_WIDTH,))


def _key_columns(a):
    pad_rank = [(0, 0)] * (a.ndim - 1) + [(0, LANE - GLA_RANK)]
    return jnp.concatenate([
        _pad_key_heads(a[..., SRC_GQ:SRC_GK]),
        _pad_key_heads(a[..., SRC_GK:SRC_GV]),
        jnp.pad(a[..., SRC_GZ:], pad_rank),
    ], axis=-1)


def kernel(x, c, w_ada, b_ada, g_mix, g_mlp, w_in, w_gate_up, b_gate, w_pool, s_pool,
           w_out, w_up, w_down, g_final):
    batch, seq, d = x.shape
    depth = w_ada.shape[0]
    tables = _tables(seq)

    c_pad = jnp.pad(c, ((0, 8 - batch), (0, 0)))
    mod_all = _ada(c_pad, w_ada, b_ada)[:, :batch].reshape(depth, batch, N_MOD, d)

    w_in_b = w_in[..., :SRC_GZ].astype(BF16)
    w_keys_b = _key_columns(w_in).astype(BF16)
    w_out_b = w_out.astype(BF16)
    w_up_b = w_up.astype(BF16)
    w_down_b = w_down.astype(BF16)
    wg_b = jnp.pad(_pad_key_heads(w_gate_up),
                   ((0, 0), (0, LANE - GLA_RANK), (0, 0))).astype(BF16)
    wp_b = w_pool.astype(BF16)
    gf = g_final.reshape(1, d)
    g_mix3 = g_mix.reshape(depth, 1, d)
    g_mlp3 = g_mlp.reshape(depth, 1, d)
    bg3 = _pad_key_heads(b_gate).reshape(depth, 1, -1)
    sp3 = s_pool.reshape(depth, 1, -1)

    x2 = x.reshape(batch * seq, d)
    for l in range(depth):
        mix = _inproj_mixer(x2, mod_all, g_mix3, w_in_b, w_keys_b, tables, wg_b, bg3, wp_b,
                            sp3, l, seq)
        x2 = _outproj(mix, x2, mod_all, w_out_b, l, seq)
        x2 = _mlp(x2, mod_all, g_mlp3, gf, w_up_b, w_down_b, l, seq,
                  final_norm=(l == depth - 1))
    return x2.reshape(batch, seq, d)
```

```python
import functools

import jax
import jax.numpy as jnp
from jax import lax
from jax.experimental import pallas as pl
from jax.experimental.pallas import tpu as pltpu

F32 = jnp.float32
BF16 = jnp.bfloat16

D_MODEL = 2048
DEPTH = 4
N_MOD = 6
EPS = 1e-6
D_FF = 4 * D_MODEL

RET_HEADS = 6
RET_DIM = 128
RET_WIDTH = RET_HEADS * RET_DIM
RET_CHUNK = 128
ROPE_BASE = 10000.0
POOL_WINDOWS = (2, 4, 8, 16)
POOL_GROUPS = 4
POOL_DIM = 128
POOL_WIDTH = POOL_GROUPS * POOL_DIM
POOL_HALO = 16
GLA_HEADS = 4
GLA_V_DIM = 192
GLA_K_DIM = 96
GLA_WIDTH = GLA_HEADS * GLA_V_DIM
GLA_KEY_WIDTH = GLA_HEADS * GLA_K_DIM
GLA_RANK = 16
GLA_TAU = 16.0
GLA_CHUNK = 64

LANE = 128

SRC_GQ = 4 * RET_WIDTH + POOL_WIDTH
SRC_GK = SRC_GQ + GLA_KEY_WIDTH
SRC_GV = SRC_GK + GLA_KEY_WIDTH
SRC_GZ = SRC_GV + 2 * GLA_WIDTH
IN_WIDTH = SRC_GZ + GLA_RANK

GLA_K_PAD = LANE
GLA_KEY_PAD_WIDTH = GLA_HEADS * GLA_K_PAD
OFF_RQ = 0
OFF_RK = OFF_RQ + RET_WIDTH
OFF_RV = OFF_RK + RET_WIDTH
OFF_RG = OFF_RV + RET_WIDTH
OFF_PU = OFF_RG + RET_WIDTH
OFF_GV = OFF_PU + POOL_WIDTH
OFF_GR = OFF_GV + GLA_WIDTH
OFF_GQ = OFF_GR + GLA_WIDTH
OFF_GK = OFF_GQ + GLA_KEY_PAD_WIDTH
OFF_GZ = OFF_GK + GLA_KEY_PAD_WIDTH
Z_WIDTH = OFF_GZ + LANE
KEYS_WIDTH = Z_WIDTH - OFF_GQ

MIX_RET = 0
MIX_POOL = RET_WIDTH
MIX_GLA = RET_WIDTH + POOL_WIDTH

GLA_WIN = 256
GLA_VWIN = (0, 128, 384, 512)

VMEM_LIMIT = 60 * 1024 * 1024

MIX_CHUNK = 128
MLP_TF = 1024


def _rms(x):
    return x * lax.rsqrt(jnp.mean(x * x, axis=-1, keepdims=True) + EPS)


def _silu(x):
    return x * jax.nn.sigmoid(x)


def _dot(a, b):
    return jnp.dot(a, b, preferred_element_type=F32)


def _dot_nt(a, b):
    return lax.dot_general(a, b, (((1,), (1,)), ((), ())), preferred_element_type=F32)


def _dot_tn(a, b):
    return lax.dot_general(a, b, (((0,), (0,)), ((), ())), preferred_element_type=F32)


def _ada_kernel(c_ref, wa_ref, wb_ref, b_ref, o_ref):
    k = pl.program_id(1)
    tk = wa_ref.shape[0]

    @pl.when(k == 0)
    def _():
        o_ref[...] = jnp.broadcast_to(b_ref[...], o_ref.shape)

    s = _silu(c_ref[...]).astype(BF16)
    o_ref[...] += (_dot(s[:, :tk], wa_ref[...].astype(BF16))
                   + _dot(s[:, tk:], wb_ref[...].astype(BF16)))


def _ada(c_pad, w_ada, b_ada, tk=128):
    depth, d, n = w_ada.shape
    rows = c_pad.shape[0]
    return pl.pallas_call(
        _ada_kernel,
        out_shape=jax.ShapeDtypeStruct((depth, rows, n), F32),
        grid=(depth, d // (2 * tk)),
        in_specs=[
            pl.BlockSpec((rows, 2 * tk), lambda l, k: (0, k)),
            pl.BlockSpec((None, tk, n), lambda l, k: (l, 2 * k, 0)),
            pl.BlockSpec((None, tk, n), lambda l, k: (l, 2 * k + 1, 0)),
            pl.BlockSpec((None, 1, n), lambda l, k: (l, 0, 0)),
        ],
        out_specs=pl.BlockSpec((None, rows, n), lambda l, k: (l, 0, 0)),
        compiler_params=pltpu.CompilerParams(
            dimension_semantics=("arbitrary", "arbitrary"),
            vmem_limit_bytes=VMEM_LIMIT),
        name="ada_mod",
    )(c_pad, w_ada, w_ada, b_ada.reshape(depth, 1, n))


def _cast_kernel(wa_ref, wb_ref, o_ref):
    half, width = wa_ref.shape[0], o_ref.shape[1]
    o_ref[:half, :] = wa_ref[:, :width].astype(BF16)
    o_ref[half:, :] = wb_ref[:, :width].astype(BF16)


def _cast_in_weights(w_in, rows=256):
    depth, d, n = w_in.shape
    half = rows // 2
    return pl.pallas_call(
        _cast_kernel,
        out_shape=jax.ShapeDtypeStruct((depth, d, SRC_GZ), BF16),
        grid=(depth, d // rows),
        in_specs=[
            pl.BlockSpec((None, half, n), lambda l, r: (l, 2 * r, 0)),
            pl.BlockSpec((None, half, n), lambda l, r: (l, 2 * r + 1, 0)),
        ],
        out_specs=pl.BlockSpec((None, rows, SRC_GZ), lambda l, r: (l, r, 0)),
        compiler_params=pltpu.CompilerParams(
            dimension_semantics=("arbitrary", "arbitrary"),
            vmem_limit_bytes=VMEM_LIMIT),
        name="cast_w_in",
    )(w_in, w_in)


def _inproj_stages(x_ref, mod_ref, g_ref, w_ref, wk_ref, z_ref, col_chunk):
    hb = []

    def norm():
        h = _rms(x_ref[...]) * g_ref[...]
        h = h * (1.0 + mod_ref[0, 1:2, :]) + mod_ref[0, 0:1, :]
        hb.append(h.astype(BF16))

    def chunk(ref, src, c0, c1):
        def run():
            z_ref[:, c0:c1] = _dot(hb[0], ref[:, src:src + c1 - c0]).astype(BF16)
        return run

    pieces = ((0, OFF_GV, w_ref, 0), (OFF_GV, OFF_GQ, w_ref, SRC_GV),
              (OFF_GQ, Z_WIDTH, wk_ref, 0))
    stages = [norm]
    for z0, z1, ref, src0 in pieces:
        for c0 in range(z0, z1, col_chunk):
            c1 = min(c0 + col_chunk, z1)
            stages.append(chunk(ref, src0 + c0 - z0, c0, c1))
    return stages


def _retention_stages(z_ref, rows, cos_ref, sin_ref, dec_ref, zeta_ref, xi_ref, gc_ref,
                      r_state, mix_ref):
    heads = range(RET_HEADS)
    cols = [slice(h * RET_DIM, (h + 1) * RET_DIM) for h in heads]
    qx, kb, scores, outs = [], [], [], []

    def rotate_and_score():
        cos = cos_ref[rows, :]
        sin = sin_ref[rows, :]
        for h in heads:
            q = z_ref[rows, OFF_RQ + cols[h].start:OFF_RQ + cols[h].stop].astype(F32)
            k = z_ref[rows, OFF_RK + cols[h].start:OFF_RK + cols[h].stop].astype(F32)
            qr = q * cos + pltpu.roll(q, RET_DIM // 2, 1) * sin
            kr = k * cos + pltpu.roll(k, RET_DIM // 2, 1) * sin
            kb.append(kr.astype(BF16))
            qx.append((qr * xi_ref[h]).astype(BF16))
            scores.append(_dot_nt(qr.astype(BF16), kb[h]))

    def attend_and_update():
        for h in heads:
            v = z_ref[rows, OFF_RV + cols[h].start:OFF_RV + cols[h].stop]
            state = r_state[h]
            lhs = jnp.concatenate([(scores[h] * dec_ref[h]).astype(BF16), qx[h]], axis=1)
            rhs = jnp.concatenate([v, state.astype(BF16)], axis=0)
            outs.append(_dot(lhs, rhs))
            vz = (v.astype(F32) * zeta_ref[h]).astype(BF16)
            r_state[h] = gc_ref[h] * state + _dot_tn(kb[h], vz)

    def norm_and_gate():
        for h in heads:
            g = z_ref[rows, OFF_RG + cols[h].start:OFF_RG + cols[h].stop].astype(F32)
            o = outs[h]
            mu = jnp.mean(o, axis=-1, keepdims=True)
            oc = o - mu
            var = jnp.mean(oc * oc, axis=-1, keepdims=True)
            out = oc * lax.rsqrt(var + EPS) * _silu(g)
            mix_ref[rows, MIX_RET + cols[h].start:MIX_RET + cols[h].stop] = out.astype(BF16)

    return [rotate_and_score, attend_and_update, norm_and_gate]


def _pool_chunk(z_ref, rows, tok0, wp_ref, sp_ref, pext, mix_ref):
    n = MIX_CHUNK
    pu = z_ref[rows, OFF_PU:OFF_PU + POOL_WIDTH].astype(F32)
    pext[POOL_HALO:POOL_HALO + n, :] = pu
    tok = tok0 + lax.broadcasted_iota(jnp.int32, (n, 1), 0)
    for grp, win in enumerate(POOL_WINDOWS):
        c0 = grp * POOL_DIM
        cur = pu[:, c0:c0 + POOL_DIM]
        acc = cur
        for s in range(1, win):
            acc = acc + pext[POOL_HALO - s:POOL_HALO - s + n, c0:c0 + POOL_DIM]
        cnt = jnp.minimum(tok + 1, win).astype(F32)
        pooled = acc / cnt - cur
        y = _dot(pooled.astype(BF16), wp_ref[grp]) * sp_ref[:, c0:c0 + POOL_DIM]
        mix_ref[rows, MIX_POOL + c0:MIX_POOL + c0 + POOL_DIM] = y.astype(BF16)
    pext[0:POOL_HALO, :] = pext[n:n + POOL_HALO, :]


def _gla_gates(z_ref, rows, wg_ref, bg_ref, ltri_ref):
    n, half = MIX_CHUNK, GLA_CHUNK
    gate = _dot(z_ref[rows, OFF_GZ:OFF_GZ + LANE], wg_ref[...]) + bg_ref[...]
    log_a = jax.nn.log_sigmoid(gate) / GLA_TAU
    la_hi = log_a.astype(BF16)
    rem = log_a - la_hi.astype(F32)
    la_mid = rem.astype(BF16)
    la_lo = (rem - la_mid.astype(F32)).astype(BF16)
    ltri = ltri_ref[...]
    b = _dot(ltri, la_hi) + _dot(ltri, la_mid) + _dot(ltri, la_lo)

    second = lax.broadcasted_iota(jnp.int32, (n, 1), 0) >= half
    b_last0 = b[half - 1:half, :]
    b_last1 = b[n - 1:n, :]
    g0 = jnp.exp(b_last0)
    g1 = jnp.exp(b_last1)
    q = z_ref[rows, OFF_GQ:OFF_GQ + GLA_KEY_PAD_WIDTH].astype(F32) * (GLA_K_DIM ** -0.5)
    k = z_ref[rows, OFF_GK:OFF_GK + GLA_KEY_PAD_WIDTH].astype(F32)
    q_dec = q * jnp.exp(b)
    k_inv = k * jnp.exp(-b)
    k_state = k * jnp.exp(jnp.where(second, b_last1, b_last0) - b)
    q_cross = jnp.where(second, q_dec * g0, q_dec)
    k_upd = jnp.where(second, k_state, k_state * g1)
    k_first = jnp.where(second, 0.0, k_state)
    g_both = g0 * g1
    return q_dec, k_inv, q_cross, k_upd, k_first, g_both


def _gla_stages(z_ref, rows, wg_ref, bg_ref, ltri_ref, s_state, mix_ref):
    n, half = MIX_CHUNK, GLA_CHUNK
    heads = range(GLA_HEADS)
    kcs = [slice(h * GLA_K_PAD, (h + 1) * GLA_K_PAD) for h in heads]
    gates, both, cross, vals = [], [], [], []

    def gate_chain():
        gates.extend(_gla_gates(z_ref, rows, wg_ref, bg_ref, ltri_ref))

    def score():
        q_dec, k_inv, q_cross, _, k_first, _ = gates
        for h in heads:
            kc = kcs[h]
            keys = jnp.concatenate(
                [k_inv[:, kc].astype(BF16), k_first[:, kc].astype(BF16)], axis=0)
            both.append(_dot_nt(q_dec[:, kc].astype(BF16), keys))
            cross.append(_dot_nt(q_cross[:, kc].astype(BF16), s_state[h].astype(BF16)))

    def attend_and_update():
        k_upd, g_both = gates[3], gates[5]
        second = lax.broadcasted_iota(jnp.int32, (n, 1), 0) >= half
        ri = lax.broadcasted_iota(jnp.int32, (n, n), 0)
        ci = lax.broadcasted_iota(jnp.int32, (n, n), 1)
        intra = ((ri >= half) == (ci >= half)) & (ri >= ci)
        for h in heads:
            kc = kcs[h]
            v0 = GLA_VWIN[h]
            v = z_ref[rows, OFF_GV + v0:OFF_GV + v0 + GLA_WIN]
            scores = (jnp.where(intra, both[h][:, :n], 0.0)
                      + jnp.where(second, both[h][:, n:], 0.0))
            vals.append(_dot(scores.astype(BF16), v) + cross[h])
            s_state[h] = s_state[h] * g_both[:, kc] + _dot_tn(v, k_upd[:, kc].astype(BF16))

    def norm_and_gate():
        lane = lax.broadcasted_iota(jnp.int32, (1, GLA_WIN), 1)
        outs = []
        for h in heads:
            v0 = GLA_VWIN[h]
            vcol = lane + v0
            vmask = (vcol >= h * GLA_V_DIM) & (vcol < (h + 1) * GLA_V_DIM)
            gr = z_ref[rows, OFF_GR + v0:OFF_GR + v0 + GLA_WIN].astype(F32)
            o = vals[h]
            ms = jnp.sum(jnp.where(vmask, o * o, 0.0), axis=-1, keepdims=True)
            on = o * lax.rsqrt(ms / GLA_V_DIM + EPS)
            outs.append(jnp.where(vmask, on * _silu(gr), 0.0))
        tiles = [
            outs[0][:, :LANE],
            outs[0][:, LANE:] + outs[1][:, :LANE],
            outs[1][:, LANE:],
            outs[2][:, :LANE],
            outs[2][:, LANE:] + outs[3][:, :LANE],
            outs[3][:, LANE:],
        ]
        for t, tile in enumerate(tiles):
            mix_ref[rows, MIX_GLA + t * LANE:MIX_GLA + (t + 1) * LANE] = tile.astype(BF16)

    return [gate_chain, score, attend_and_update, norm_and_gate]


def _interleave(major, minor):
    order, placed = [], 0
    for i, stage in enumerate(major):
        order.append(stage)
        due = (i + 1) * len(minor) // len(major)
        order.extend(minor[placed:due])
        placed = due
    return order


def _inproj_mixer_step(x_ref, mod_ref, g_ref, w_ref, wk_ref, z_write, z_read, cos_ref,
                       sin_ref, dec_ref, zeta_ref, xi_ref, gc_ref, wg_ref, bg_ref, wp_ref,
                       sp_ref, ltri_ref, mix_ref, r_state, s_state, pext, tok_base,
                       col_chunk):
    project = _inproj_stages(x_ref, mod_ref, g_ref, w_ref, wk_ref, z_write, col_chunk)
    mix = []
    for c in range(z_read.shape[0] // MIX_CHUNK):
        rows = pl.ds(c * MIX_CHUNK, MIX_CHUNK)
        gate_chain, g_score, g_attend, g_out = _gla_stages(
            z_read, rows, wg_ref, bg_ref, ltri_ref, s_state, mix_ref)
        r_score, r_attend, r_out = _retention_stages(
            z_read, rows, cos_ref, sin_ref, dec_ref, zeta_ref, xi_ref, gc_ref, r_state,
            mix_ref)
        pool = functools.partial(_pool_chunk, z_read, rows, tok_base + c * MIX_CHUNK,
                                 wp_ref, sp_ref, pext, mix_ref)
        mix += [gate_chain, r_score, r_attend, g_score, r_out, g_attend, g_out, pool]
    for stage in _interleave(project, mix):
        stage()


def _inproj_mixer_kernel(x_ref, mod_ref, g_ref, w_ref, wk_ref, cos_ref, sin_ref, dec_ref,
                         zeta_ref, xi_ref, gc_ref, wg_ref, bg_ref, wp_ref, sp_ref,
                         ltri_ref, mix_ref, z_even, z_odd, r_state, s_state, pext, *,
                         blocks_per_seq, col_chunk):
    j = pl.program_id(0)
    tb = x_ref.shape[0]
    sblk = lax.rem(jnp.maximum(j - 1, 0), blocks_per_seq)

    @pl.when(j == 0)
    def _():
        z_odd[...] = jnp.zeros_like(z_odd)

    @pl.when(sblk == 0)
    def _():
        r_state[...] = jnp.zeros_like(r_state)
        s_state[...] = jnp.zeros_like(s_state)
        pext[...] = jnp.zeros_like(pext)

    step = functools.partial(
        _inproj_mixer_step, cos_ref=cos_ref, sin_ref=sin_ref, dec_ref=dec_ref,
        zeta_ref=zeta_ref, xi_ref=xi_ref, gc_ref=gc_ref, wg_ref=wg_ref, bg_ref=bg_ref,
        wp_ref=wp_ref, sp_ref=sp_ref, ltri_ref=ltri_ref, mix_ref=mix_ref,
        r_state=r_state, s_state=s_state, pext=pext, tok_base=sblk * tb,
        col_chunk=col_chunk)

    @pl.when(j % 2 == 0)
    def _():
        step(x_ref, mod_ref, g_ref, w_ref, wk_ref, z_even, z_odd)

    @pl.when(j % 2 == 1)
    def _():
        step(x_ref, mod_ref, g_ref, w_ref, wk_ref, z_odd, z_even)


def _inproj_mixer(x2, mod_all, g_all, w_all, wk_all, tables, wg, bg, wp, sp, layer, seq,
                  tb=256, col_chunk=256):
    t, d = x2.shape
    nblk = t // tb
    per_seq = seq // tb
    cos2, sin2, dec, zeta, xi, gc, ltri = tables
    cur = lambda j: jnp.minimum(j, nblk - 1)
    prev = lambda j: jnp.maximum(j - 1, 0)
    const = lambda a: pl.BlockSpec(a.shape, lambda j: (0,) * a.ndim,
                                   pipeline_mode=pl.Buffered(1))
    per_layer = lambda a: pl.BlockSpec(
        (None,) + a.shape[1:], lambda j: (layer,) + (0,) * (a.ndim - 1),
        pipeline_mode=pl.Buffered(1))
    return pl.pallas_call(
        functools.partial(_inproj_mixer_kernel, blocks_per_seq=per_seq,
                          col_chunk=col_chunk),
        out_shape=jax.ShapeDtypeStruct((t, D_MODEL), BF16),
        grid=(nblk + 1,),
        in_specs=[
            pl.BlockSpec((tb, d), lambda j: (cur(j), 0)),
            pl.BlockSpec((None, 1, N_MOD, d), lambda j: (layer, cur(j) // per_seq, 0, 0)),
            pl.BlockSpec((None, 1, d), lambda j: (layer, 0, 0)),
            per_layer(w_all), per_layer(wk_all),
            pl.BlockSpec((tb, RET_DIM), lambda j: (prev(j) % per_seq, 0)),
            pl.BlockSpec((tb, RET_DIM), lambda j: (prev(j) % per_seq, 0)),
            const(dec), const(zeta), const(xi), const(gc),
            per_layer(wg), per_layer(bg), per_layer(wp), per_layer(sp),
            const(ltri),
        ],
        out_specs=pl.BlockSpec((tb, D_MODEL), lambda j: (prev(j), 0)),
        scratch_shapes=[
            pltpu.VMEM((tb, Z_WIDTH), BF16),
            pltpu.VMEM((tb, Z_WIDTH), BF16),
            pltpu.VMEM((RET_HEADS, RET_DIM, RET_DIM), F32),
            pltpu.VMEM((GLA_HEADS, GLA_WIN, GLA_K_PAD), F32),
            pltpu.VMEM((POOL_HALO + MIX_CHUNK, POOL_WIDTH), F32),
        ],
        compiler_params=pltpu.CompilerParams(
            dimension_semantics=("arbitrary",),
            vmem_limit_bytes=VMEM_LIMIT),
        name="inproj_mixer",
    )(x2, mod_all, g_all, w_all, wk_all, cos2, sin2, dec, zeta, xi, gc, wg, bg, wp, sp,
      ltri)


def _outproj_kernel(mix_ref, x_ref, mod_ref, w_ref, o_ref):
    y = _dot(mix_ref[...], w_ref[...])
    o_ref[...] = x_ref[...] + mod_ref[0, 2:3, :] * y


def _outproj(mix, x2, mod_all, w_all, layer, seq, tm=512):
    t, d = x2.shape
    per_batch = seq // tm
    return pl.pallas_call(
        _outproj_kernel,
        out_shape=jax.ShapeDtypeStruct((t, d), F32),
        grid=(t // tm,),
        in_specs=[
            pl.BlockSpec((tm, d), lambda i: (i, 0)),
            pl.BlockSpec((tm, d), lambda i: (i, 0)),
            pl.BlockSpec((None, 1, N_MOD, d), lambda i: (layer, i // per_batch, 0, 0)),
            pl.BlockSpec((None, d, d), lambda i: (layer, 0, 0),
                         pipeline_mode=pl.Buffered(1)),
        ],
        out_specs=pl.BlockSpec((tm, d), lambda i: (i, 0)),
        compiler_params=pltpu.CompilerParams(
            dimension_semantics=("arbitrary",),
            vmem_limit_bytes=VMEM_LIMIT),
        name="out_proj",
    )(mix, x2, mod_all, w_all)


def _mlp_norm(x, mod_ref, g_ref):
    h = _rms(x) * g_ref[...]
    return (h * (1.0 + mod_ref[0, 4:5, :]) + mod_ref[0, 3:4, :]).astype(BF16)


def _mlp_kernel(x_ref, xn_ref, mod_ref, modn_ref, g_ref, gf_ref, wu0_ref, wu1_ref,
                wd0_ref, wd1_ref, o_ref, h_scr, *, final_norm):
    i = pl.program_id(0)
    f = pl.program_id(1)
    nf = pl.num_programs(1)
    tm = x_ref.shape[0]
    rows_per_step = tm // nf
    cur = i % 2

    @pl.when((i == 0) & (f == 0))
    def _():
        h_scr[0] = _mlp_norm(x_ref[...], mod_ref, g_ref)

    def step(first, last):
        dh = wu0_ref.shape[0]
        a = (_dot(h_scr[cur, :, :dh], wu0_ref[...])
             + _dot(h_scr[cur, :, dh:], wu1_ref[...]))
        r = jnp.square(jnp.maximum(a, 0.0)).astype(BF16)
        th = wd0_ref.shape[0]
        y = _dot(r[:, :th], wd0_ref[...]) + _dot(r[:, th:], wd1_ref[...])
        if first:
            o_ref[...] = y
        elif not last:
            o_ref[...] += y
        else:
            xo = x_ref[...] + mod_ref[0, 5:6, :] * (o_ref[...] + y)
            if final_norm:
                xo = _rms(xo) * gf_ref[...]
            o_ref[...] = xo
        r0 = pl.multiple_of(f * rows_per_step, rows_per_step)
        h_scr[1 - cur, pl.ds(r0, rows_per_step), :] = _mlp_norm(
            xn_ref[pl.ds(r0, rows_per_step), :], modn_ref, g_ref)

    pl.when(f == 0)(functools.partial(step, True, False))
    pl.when((f > 0) & (f < nf - 1))(functools.partial(step, False, False))
    pl.when(f == nf - 1)(functools.partial(step, False, True))


def _mlp(x2, mod_all, g_all, gf, wu_all, wd_all, layer, seq, final_norm, tm=512,
         tf=MLP_TF):
    t, d = x2.shape
    nf = wu_all.shape[2] // tf
    assert nf >= 2, "the first and last hidden tiles take different code paths"
    per_batch = seq // tm
    last_blk = t // tm - 1
    nxt = lambda i: jnp.minimum(i + 1, last_blk)
    return pl.pallas_call(
        functools.partial(_mlp_kernel, final_norm=final_norm),
        out_shape=jax.ShapeDtypeStruct((t, d), F32),
        grid=(t // tm, nf),
        in_specs=[
            pl.BlockSpec((tm, d), lambda i, f: (i, 0)),
            pl.BlockSpec((tm, d), lambda i, f: (nxt(i), 0)),
            pl.BlockSpec((None, 1, N_MOD, d), lambda i, f: (layer, i // per_batch, 0, 0)),
            pl.BlockSpec((None, 1, N_MOD, d),
                         lambda i, f: (layer, nxt(i) // per_batch, 0, 0)),
            pl.BlockSpec((None, 1, d), lambda i, f: (layer, 0, 0)),
            pl.BlockSpec((1, d), lambda i, f: (0, 0)),
            pl.BlockSpec((None, d // 2, tf), lambda i, f: (layer, 0, f)),
            pl.BlockSpec((None, d // 2, tf), lambda i, f: (layer, 1, f)),
            pl.BlockSpec((None, tf // 2, d), lambda i, f: (layer, 2 * f, 0)),
            pl.BlockSpec((None, tf // 2, d), lambda i, f: (layer, 2 * f + 1, 0)),
        ],
        out_specs=pl.BlockSpec((tm, d), lambda i, f: (i, 0)),
        scratch_shapes=[pltpu.VMEM((2, tm, d), BF16)],
        compiler_params=pltpu.CompilerParams(
            dimension_semantics=("arbitrary", "arbitrary"),
            vmem_limit_bytes=VMEM_LIMIT),
        name="mlp",
    )(x2, x2, mod_all, mod_all, g_all, gf, wu_all, wu_all, wd_all, wd_all)


def _tables(seq):
    pos = jnp.arange(seq, dtype=F32)
    inv_freq = ROPE_BASE ** (-jnp.arange(0, RET_DIM, 2, dtype=F32) / RET_DIM)
    ang = pos[:, None] * inv_freq[None, :]
    cos, sin = jnp.cos(ang), jnp.sin(ang)
    cos2 = jnp.concatenate([cos, cos], axis=-1)
    sin2 = jnp.concatenate([-sin, sin], axis=-1)

    c = RET_CHUNK
    log_g = jnp.log1p(-jnp.exp2(-5.0 - jnp.arange(RET_HEADS, dtype=F32)))
    idx = jnp.arange(c, dtype=F32)
    rel = idx[:, None] - idx[None, :]
    decay = jnp.where(rel[None] >= 0,
                      jnp.exp(jnp.maximum(rel, 0.0)[None] * log_g[:, None, None]), 0.0)
    scale = RET_DIM ** -0.5
    dec = decay * scale
    zeta = jnp.exp((c - 1.0 - idx)[None, :] * log_g[:, None])
    zeta = jnp.broadcast_to(zeta[:, :, None], (RET_HEADS, c, RET_DIM))
    xi = jnp.exp((idx + 1.0)[None, :] * log_g[:, None]) * scale
    xi = jnp.broadcast_to(xi[:, :, None], (RET_HEADS, c, RET_DIM))
    gc = jnp.broadcast_to(jnp.exp(c * log_g)[:, None, None], (RET_HEADS, 1, RET_DIM))

    r = jnp.arange(MIX_CHUNK)
    same = (r[:, None] // GLA_CHUNK) == (r[None, :] // GLA_CHUNK)
    ltri = jnp.where(same & (r[:, None] >= r[None, :]), 1.0, 0.0).astype(BF16)
    return cos2, sin2, dec, zeta, xi, gc, ltri


def _pad_key_heads(a):
    lead = a.shape[:-1]
    a = a.reshape(lead + (GLA_HEADS, GLA_K_DIM))
    a = jnp.pad(a, [(0, 0)] * (len(lead) + 1) + [(0, GLA_K_PAD - GLA_K_DIM)])
    return a.reshape(lead + (GLA_KEY_PAD_WIDTH,))


def _key_columns(a):
    pad_rank = [(0, 0)] * (a.ndim - 1) + [(0, LANE - GLA_RANK)]
    return jnp.concatenate([
        _pad_key_heads(a[..., SRC_GQ:SRC_GK]),
        _pad_key_heads(a[..., SRC_GK:SRC_GV]),
        jnp.pad(a[..., SRC_GZ:], pad_rank),
    ], axis=-1)


def kernel(x, c, w_ada, b_ada, g_mix, g_mlp, w_in, w_gate_up, b_gate, w_pool, s_pool,
           w_out, w_up, w_down, g_final):
    batch, seq, d = x.shape
    depth = w_ada.shape[0]
    tables = _tables(seq)

    c_pad = jnp.pad(c, ((0, 8 - batch), (0, 0)))
    mod_all = _ada(c_pad, w_ada, b_ada)[:, :batch].reshape(depth, batch, N_MOD, d)

    w_in_b = _cast_in_weights(w_in)
    w_keys_b = _key_columns(w_in).astype(BF16)
    w_out_b = w_out.astype(BF16)
    w_up_b = w_up.astype(BF16)
    w_down_b = w_down.astype(BF16)
    wg_b = jnp.pad(_pad_key_heads(w_gate_up),
                   ((0, 0), (0, LANE - GLA_RANK), (0, 0))).astype(BF16)
    wp_b = w_pool.astype(BF16)
    gf = g_final.reshape(1, d)
    g_mix3 = g_mix.reshape(depth, 1, d)
    g_mlp3 = g_mlp.reshape(depth, 1, d)
    bg3 = _pad_key_heads(b_gate).reshape(depth, 1, -1)
    sp3 = s_pool.reshape(depth, 1, -1)

    x2 = x.reshape(batch * seq, d)
    for l in range(depth):
        mix = _inproj_mixer(x2, mod_all, g_mix3, w_in_b, w_keys_b, tables, wg_b, bg3, wp_b,
                            sp3, l, seq)
        x2 = _outproj(mix, x2, mod_all, w_out_b, l, seq)
        x2 = _mlp(x2, mod_all, g_mlp3, gf, w_up_b, w_down_b, l, seq,
                  final_norm=(l == depth - 1))
    return x2.reshape(batch, seq, d)
```

```python
import functools

import jax
import jax.numpy as jnp
from jax import lax
from jax.experimental import pallas as pl
from jax.experimental.pallas import tpu as pltpu

F32 = jnp.float32
BF16 = jnp.bfloat16

D_MODEL = 2048
DEPTH = 4
N_MOD = 6
EPS = 1e-6
D_FF = 4 * D_MODEL

RET_HEADS = 6
RET_DIM = 128
RET_WIDTH = RET_HEADS * RET_DIM
RET_CHUNK = 128
ROPE_BASE = 10000.0
POOL_WINDOWS = (2, 4, 8, 16)
POOL_GROUPS = 4
POOL_DIM = 128
POOL_WIDTH = POOL_GROUPS * POOL_DIM
POOL_HALO = 16
GLA_HEADS = 4
GLA_V_DIM = 192
GLA_K_DIM = 96
GLA_WIDTH = GLA_HEADS * GLA_V_DIM
GLA_KEY_WIDTH = GLA_HEADS * GLA_K_DIM
GLA_RANK = 16
GLA_TAU = 16.0
GLA_CHUNK = 64

LANE = 128

SRC_GQ = 4 * RET_WIDTH + POOL_WIDTH
SRC_GK = SRC_GQ + GLA_KEY_WIDTH
SRC_GV = SRC_GK + GLA_KEY_WIDTH
SRC_GZ = SRC_GV + 2 * GLA_WIDTH
IN_WIDTH = SRC_GZ + GLA_RANK

GLA_K_PAD = LANE
GLA_KEY_PAD_WIDTH = GLA_HEADS * GLA_K_PAD
OFF_RQ = 0
OFF_RK = OFF_RQ + RET_WIDTH
OFF_RV = OFF_RK + RET_WIDTH
OFF_RG = OFF_RV + RET_WIDTH
OFF_PU = OFF_RG + RET_WIDTH
OFF_GV = OFF_PU + POOL_WIDTH
OFF_GR = OFF_GV + GLA_WIDTH
OFF_GQ = OFF_GR + GLA_WIDTH
OFF_GK = OFF_GQ + GLA_KEY_PAD_WIDTH
OFF_GZ = OFF_GK + GLA_KEY_PAD_WIDTH
Z_WIDTH = OFF_GZ + LANE
KEYS_WIDTH = Z_WIDTH - OFF_GQ

MIX_RET = 0
MIX_POOL = RET_WIDTH
MIX_GLA = RET_WIDTH + POOL_WIDTH

GLA_WIN = 256
GLA_VWIN = (0, 128, 384, 512)

VMEM_LIMIT = 60 * 1024 * 1024

MIX_CHUNK = 128
MLP_TF = 1024


def _rms(x):
    return x * lax.rsqrt(jnp.mean(x * x, axis=-1, keepdims=True) + EPS)


def _silu(x):
    return x * jax.nn.sigmoid(x)


def _dot(a, b):
    return jnp.dot(a, b, preferred_element_type=F32)


def _dot_nt(a, b):
    return lax.dot_general(a, b, (((1,), (1,)), ((), ())), preferred_element_type=F32)


def _dot_tn(a, b):
    return lax.dot_general(a, b, (((0,), (0,)), ((), ())), preferred_element_type=F32)


def _ada_kernel(c_ref, wa_ref, wb_ref, b_ref, o_ref):
    k = pl.program_id(1)
    tk = wa_ref.shape[0]

    @pl.when(k == 0)
    def _():
        o_ref[...] = jnp.broadcast_to(b_ref[...], o_ref.shape)

    s = _silu(c_ref[...]).astype(BF16)
    o_ref[...] += (_dot(s[:, :tk], wa_ref[...].astype(BF16))
                   + _dot(s[:, tk:], wb_ref[...].astype(BF16)))


def _ada(c_pad, w_ada, b_ada, tk=128):
    depth, d, n = w_ada.shape
    rows = c_pad.shape[0]
    return pl.pallas_call(
        _ada_kernel,
        out_shape=jax.ShapeDtypeStruct((depth, rows, n), F32),
        grid=(depth, d // (2 * tk)),
        in_specs=[
            pl.BlockSpec((rows, 2 * tk), lambda l, k: (0, k)),
            pl.BlockSpec((None, tk, n), lambda l, k: (l, 2 * k, 0)),
            pl.BlockSpec((None, tk, n), lambda l, k: (l, 2 * k + 1, 0)),
            pl.BlockSpec((None, 1, n), lambda l, k: (l, 0, 0)),
        ],
        out_specs=pl.BlockSpec((None, rows, n), lambda l, k: (l, 0, 0)),
        compiler_params=pltpu.CompilerParams(
            dimension_semantics=("arbitrary", "arbitrary"),
            vmem_limit_bytes=VMEM_LIMIT),
        name="ada_mod",
    )(c_pad, w_ada, w_ada, b_ada.reshape(depth, 1, n))


def _inproj_stages(x_ref, mod_ref, g_ref, w_ref, wk_ref, z_ref, col_chunk):
    hb = []

    def norm():
        h = _rms(x_ref[...]) * g_ref[...]
        h = h * (1.0 + mod_ref[0, 1:2, :]) + mod_ref[0, 0:1, :]
        hb.append(h.astype(BF16))

    def chunk(ref, src, c0, c1):
        def run():
            z_ref[:, c0:c1] = _dot(hb[0], ref[:, src:src + c1 - c0]).astype(BF16)
        return run

    pieces = ((0, OFF_GV, w_ref, 0), (OFF_GV, OFF_GQ, w_ref, SRC_GV),
              (OFF_GQ, Z_WIDTH, wk_ref, 0))
    stages = [norm]
    for z0, z1, ref, src0 in pieces:
        for c0 in range(z0, z1, col_chunk):
            c1 = min(c0 + col_chunk, z1)
            stages.append(chunk(ref, src0 + c0 - z0, c0, c1))
    return stages


def _retention_stages(z_ref, rows, cos_ref, sin_ref, dec_ref, zeta_ref, xi_ref, gc_ref,
                      r_state, mix_ref):
    heads = range(RET_HEADS)
    cols = [slice(h * RET_DIM, (h + 1) * RET_DIM) for h in heads]
    qx, kb, scores, outs = [], [], [], []

    def rotate_and_score():
        cos = cos_ref[rows, :]
        sin = sin_ref[rows, :]
        for h in heads:
            q = z_ref[rows, OFF_RQ + cols[h].start:OFF_RQ + cols[h].stop].astype(F32)
            k = z_ref[rows, OFF_RK + cols[h].start:OFF_RK + cols[h].stop].astype(F32)
            qr = q * cos + pltpu.roll(q, RET_DIM // 2, 1) * sin
            kr = k * cos + pltpu.roll(k, RET_DIM // 2, 1) * sin
            kb.append(kr.astype(BF16))
            qx.append((qr * xi_ref[h]).astype(BF16))
            scores.append(_dot_nt(qr.astype(BF16), kb[h]))

    def attend_and_update():
        for h in heads:
            v = z_ref[rows, OFF_RV + cols[h].start:OFF_RV + cols[h].stop]
            state = r_state[h]
            lhs = jnp.concatenate([(scores[h] * dec_ref[h]).astype(BF16), qx[h]], axis=1)
            rhs = jnp.concatenate([v, state.astype(BF16)], axis=0)
            outs.append(_dot(lhs, rhs))
            vz = (v.astype(F32) * zeta_ref[h]).astype(BF16)
            r_state[h] = gc_ref[h] * state + _dot_tn(kb[h], vz)

    def norm_and_gate():
        for h in heads:
            g = z_ref[rows, OFF_RG + cols[h].start:OFF_RG + cols[h].stop].astype(F32)
            o = outs[h]
            mu = jnp.mean(o, axis=-1, keepdims=True)
            oc = o - mu
            var = jnp.mean(oc * oc, axis=-1, keepdims=True)
            out = oc * lax.rsqrt(var + EPS) * _silu(g)
            mix_ref[rows, MIX_RET + cols[h].start:MIX_RET + cols[h].stop] = out.astype(BF16)

    return [rotate_and_score, attend_and_update, norm_and_gate]


def _pool_chunk(z_ref, rows, tok0, wp_ref, sp_ref, pext, mix_ref):
    n = MIX_CHUNK
    pu = z_ref[rows, OFF_PU:OFF_PU + POOL_WIDTH].astype(F32)
    pext[POOL_HALO:POOL_HALO + n, :] = pu
    tok = tok0 + lax.broadcasted_iota(jnp.int32, (n, 1), 0)
    for grp, win in enumerate(POOL_WINDOWS):
        c0 = grp * POOL_DIM
        cur = pu[:, c0:c0 + POOL_DIM]
        acc = cur
        for s in range(1, win):
            acc = acc + pext[POOL_HALO - s:POOL_HALO - s + n, c0:c0 + POOL_DIM]
        cnt = jnp.minimum(tok + 1, win).astype(F32)
        pooled = acc / cnt - cur
        y = _dot(pooled.astype(BF16), wp_ref[grp]) * sp_ref[:, c0:c0 + POOL_DIM]
        mix_ref[rows, MIX_POOL + c0:MIX_POOL + c0 + POOL_DIM] = y.astype(BF16)
    pext[0:POOL_HALO, :] = pext[n:n + POOL_HALO, :]


def _gla_gates(z_ref, rows, wg_ref, bg_ref, ltri_ref):
    n, half = MIX_CHUNK, GLA_CHUNK
    gate = _dot(z_ref[rows, OFF_GZ:OFF_GZ + LANE], wg_ref[...]) + bg_ref[...]
    log_a = jax.nn.log_sigmoid(gate) / GLA_TAU
    la_hi = log_a.astype(BF16)
    rem = log_a - la_hi.astype(F32)
    la_mid = rem.astype(BF16)
    la_lo = (rem - la_mid.astype(F32)).astype(BF16)
    ltri = ltri_ref[...]
    b = _dot(ltri, la_hi) + _dot(ltri, la_mid) + _dot(ltri, la_lo)

    second = lax.broadcasted_iota(jnp.int32, (n, 1), 0) >= half
    b_last0 = b[half - 1:half, :]
    b_last1 = b[n - 1:n, :]
    g0 = jnp.exp(b_last0)
    g1 = jnp.exp(b_last1)
    q = z_ref[rows, OFF_GQ:OFF_GQ + GLA_KEY_PAD_WIDTH].astype(F32) * (GLA_K_DIM ** -0.5)
    k = z_ref[rows, OFF_GK:OFF_GK + GLA_KEY_PAD_WIDTH].astype(F32)
    q_dec = q * jnp.exp(b)
    k_inv = k * jnp.exp(-b)
    k_state = k * jnp.exp(jnp.where(second, b_last1, b_last0) - b)
    q_cross = jnp.where(second, q_dec * g0, q_dec)
    k_upd = jnp.where(second, k_state, k_state * g1)
    k_first = jnp.where(second, 0.0, k_state)
    g_both = g0 * g1
    return q_dec, k_inv, q_cross, k_upd, k_first, g_both


def _gla_stages(z_ref, rows, wg_ref, bg_ref, ltri_ref, s_state, mix_ref):
    n, half = MIX_CHUNK, GLA_CHUNK
    heads = range(GLA_HEADS)
    kcs = [slice(h * GLA_K_PAD, (h + 1) * GLA_K_PAD) for h in heads]
    gates, both, cross, vals = [], [], [], []

    def gate_chain():
        gates.extend(_gla_gates(z_ref, rows, wg_ref, bg_ref, ltri_ref))

    def score():
        q_dec, k_inv, q_cross, _, k_first, _ = gates
        for h in heads:
            kc = kcs[h]
            keys = jnp.concatenate(
                [k_inv[:, kc].astype(BF16), k_first[:, kc].astype(BF16)], axis=0)
            both.append(_dot_nt(q_dec[:, kc].astype(BF16), keys))
            cross.append(_dot_nt(q_cross[:, kc].astype(BF16), s_state[h].astype(BF16)))

    def attend_and_update():
        k_upd, g_both = gates[3], gates[5]
        second = lax.broadcasted_iota(jnp.int32, (n, 1), 0) >= half
        ri = lax.broadcasted_iota(jnp.int32, (n, n), 0)
        ci = lax.broadcasted_iota(jnp.int32, (n, n), 1)
        intra = ((ri >= half) == (ci >= half)) & (ri >= ci)
        for h in heads:
            kc = kcs[h]
            v0 = GLA_VWIN[h]
            v = z_ref[rows, OFF_GV + v0:OFF_GV + v0 + GLA_WIN]
            scores = (jnp.where(intra, both[h][:, :n], 0.0)
                      + jnp.where(second, both[h][:, n:], 0.0))
            vals.append(_dot(scores.astype(BF16), v) + cross[h])
            s_state[h] = s_state[h] * g_both[:, kc] + _dot_tn(v, k_upd[:, kc].astype(BF16))

    def norm_and_gate():
        lane = lax.broadcasted_iota(jnp.int32, (1, GLA_WIN), 1)
        outs = []
        for h in heads:
            v0 = GLA_VWIN[h]
            vcol = lane + v0
            vmask = (vcol >= h * GLA_V_DIM) & (vcol < (h + 1) * GLA_V_DIM)
            gr = z_ref[rows, OFF_GR + v0:OFF_GR + v0 + GLA_WIN].astype(F32)
            o = vals[h]
            ms = jnp.sum(jnp.where(vmask, o * o, 0.0), axis=-1, keepdims=True)
            on = o * lax.rsqrt(ms / GLA_V_DIM + EPS)
            outs.append(jnp.where(vmask, on * _silu(gr), 0.0))
        tiles = [
            outs[0][:, :LANE],
            outs[0][:, LANE:] + outs[1][:, :LANE],
            outs[1][:, LANE:],
            outs[2][:, :LANE],
            outs[2][:, LANE:] + outs[3][:, :LANE],
            outs[3][:, LANE:],
        ]
        for t, tile in enumerate(tiles):
            mix_ref[rows, MIX_GLA + t * LANE:MIX_GLA + (t + 1) * LANE] = tile.astype(BF16)

    return [gate_chain, score, attend_and_update, norm_and_gate]


def _interleave(major, minor):
    order, placed = [], 0
    for i, stage in enumerate(major):
        order.append(stage)
        due = (i + 1) * len(minor) // len(major)
        order.extend(minor[placed:due])
        placed = due
    return order


def _inproj_mixer_step(x_ref, mod_ref, g_ref, w_ref, wk_ref, z_write, z_read, cos_ref,
                       sin_ref, dec_ref, zeta_ref, xi_ref, gc_ref, wg_ref, bg_ref, wp_ref,
                       sp_ref, ltri_ref, mix_ref, r_state, s_state, pext, tok_base,
                       col_chunk):
    project = _inproj_stages(x_ref, mod_ref, g_ref, w_ref, wk_ref, z_write, col_chunk)
    mix = []
    for c in range(z_read.shape[0] // MIX_CHUNK):
        rows = pl.ds(c * MIX_CHUNK, MIX_CHUNK)
        gate_chain, g_score, g_attend, g_out = _gla_stages(
            z_read, rows, wg_ref, bg_ref, ltri_ref, s_state, mix_ref)
        r_score, r_attend, r_out = _retention_stages(
            z_read, rows, cos_ref, sin_ref, dec_ref, zeta_ref, xi_ref, gc_ref, r_state,
            mix_ref)
        pool = functools.partial(_pool_chunk, z_read, rows, tok_base + c * MIX_CHUNK,
                                 wp_ref, sp_ref, pext, mix_ref)
        mix += [gate_chain, r_score, r_attend, g_score, r_out, g_attend, g_out, pool]
    for stage in _interleave(project, mix):
        stage()


def _inproj_mixer_kernel(x_ref, mod_ref, g_ref, w_ref, wk_ref, cos_ref, sin_ref, dec_ref,
                         zeta_ref, xi_ref, gc_ref, wg_ref, bg_ref, wp_ref, sp_ref,
                         ltri_ref, mix_ref, z_even, z_odd, r_state, s_state, pext, *,
                         blocks_per_seq, col_chunk):
    j = pl.program_id(0)
    tb = x_ref.shape[0]
    sblk = lax.rem(jnp.maximum(j - 1, 0), blocks_per_seq)

    @pl.when(j == 0)
    def _():
        z_odd[...] = jnp.zeros_like(z_odd)

    @pl.when(sblk == 0)
    def _():
        r_state[...] = jnp.zeros_like(r_state)
        s_state[...] = jnp.zeros_like(s_state)
        pext[...] = jnp.zeros_like(pext)

    step = functools.partial(
        _inproj_mixer_step, cos_ref=cos_ref, sin_ref=sin_ref, dec_ref=dec_ref,
        zeta_ref=zeta_ref, xi_ref=xi_ref, gc_ref=gc_ref, wg_ref=wg_ref, bg_ref=bg_ref,
        wp_ref=wp_ref, sp_ref=sp_ref, ltri_ref=ltri_ref, mix_ref=mix_ref,
        r_state=r_state, s_state=s_state, pext=pext, tok_base=sblk * tb,
        col_chunk=col_chunk)

    @pl.when(j % 2 == 0)
    def _():
        step(x_ref, mod_ref, g_ref, w_ref, wk_ref, z_even, z_odd)

    @pl.when(j % 2 == 1)
    def _():
        step(x_ref, mod_ref, g_ref, w_ref, wk_ref, z_odd, z_even)


def _inproj_mixer(x2, mod_all, g_all, w_all, wk_all, tables, wg, bg, wp, sp, layer, seq,
                  tb=256, col_chunk=256):
    t, d = x2.shape
    nblk = t // tb
    per_seq = seq // tb
    cos2, sin2, dec, zeta, xi, gc, ltri = tables
    cur = lambda j: jnp.minimum(j, nblk - 1)
    prev = lambda j: jnp.maximum(j - 1, 0)
    const = lambda a: pl.BlockSpec(a.shape, lambda j: (0,) * a.ndim,
                                   pipeline_mode=pl.Buffered(1))
    per_layer = lambda a: pl.BlockSpec(
        (None,) + a.shape[1:], lambda j: (layer,) + (0,) * (a.ndim - 1),
        pipeline_mode=pl.Buffered(1))
    return pl.pallas_call(
        functools.partial(_inproj_mixer_kernel, blocks_per_seq=per_seq,
                          col_chunk=col_chunk),
        out_shape=jax.ShapeDtypeStruct((t, D_MODEL), BF16),
        grid=(nblk + 1,),
        in_specs=[
            pl.BlockSpec((tb, d), lambda j: (cur(j), 0)),
            pl.BlockSpec((None, 1, N_MOD, d), lambda j: (layer, cur(j) // per_seq, 0, 0)),
            pl.BlockSpec((None, 1, d), lambda j: (layer, 0, 0)),
            per_layer(w_all), per_layer(wk_all),
            pl.BlockSpec((tb, RET_DIM), lambda j: (prev(j) % per_seq, 0)),
            pl.BlockSpec((tb, RET_DIM), lambda j: (prev(j) % per_seq, 0)),
            const(dec), const(zeta), const(xi), const(gc),
            per_layer(wg), per_layer(bg), per_layer(wp), per_layer(sp),
            const(ltri),
        ],
        out_specs=pl.BlockSpec((tb, D_MODEL), lambda j: (prev(j), 0)),
        scratch_shapes=[
            pltpu.VMEM((tb, Z_WIDTH), BF16),
            pltpu.VMEM((tb, Z_WIDTH), BF16),
            pltpu.VMEM((RET_HEADS, RET_DIM, RET_DIM), F32),
            pltpu.VMEM((GLA_HEADS, GLA_WIN, GLA_K_PAD), F32),
            pltpu.VMEM((POOL_HALO + MIX_CHUNK, POOL_WIDTH), F32),
        ],
        compiler_params=pltpu.CompilerParams(
            dimension_semantics=("arbitrary",),
            vmem_limit_bytes=VMEM_LIMIT),
        name="inproj_mixer",
    )(x2, mod_all, g_all, w_all, wk_all, cos2, sin2, dec, zeta, xi, gc, wg, bg, wp, sp,
      ltri)


def _outproj_kernel(mix_ref, x_ref, mod_ref, w_ref, o_ref):
    y = _dot(mix_ref[...], w_ref[...])
    o_ref[...] = x_ref[...] + mod_ref[0, 2:3, :] * y


def _outproj(mix, x2, mod_all, w_all, layer, seq, tm=512):
    t, d = x2.shape
    per_batch = seq // tm
    return pl.pallas_call(
        _outproj_kernel,
        out_shape=jax.ShapeDtypeStruct((t, d), F32),
        grid=(t // tm,),
        in_specs=[
            pl.BlockSpec((tm, d), lambda i: (i, 0)),
            pl.BlockSpec((tm, d), lambda i: (i, 0)),
            pl.BlockSpec((None, 1, N_MOD, d), lambda i: (layer, i // per_batch, 0, 0)),
            pl.BlockSpec((None, d, d), lambda i: (layer, 0, 0),
                         pipeline_mode=pl.Buffered(1)),
        ],
        out_specs=pl.BlockSpec((tm, d), lambda i: (i, 0)),
        compiler_params=pltpu.CompilerParams(
            dimension_semantics=("arbitrary",),
            vmem_limit_bytes=VMEM_LIMIT),
        name="out_proj",
    )(mix, x2, mod_all, w_all)


def _mlp_norm(x, mod_ref, g_ref):
    h = _rms(x) * g_ref[...]
    return (h * (1.0 + mod_ref[0, 4:5, :]) + mod_ref[0, 3:4, :]).astype(BF16)


def _mlp_kernel(x_ref, xn_ref, mod_ref, modn_ref, g_ref, gf_ref, wu0_ref, wu1_ref,
                wd0_ref, wd1_ref, o_ref, h_scr, *, final_norm):
    i = pl.program_id(0)
    f = pl.program_id(1)
    nf = pl.num_programs(1)
    tm = x_ref.shape[0]
    rows_per_step = tm // nf
    cur = i % 2

    @pl.when((i == 0) & (f == 0))
    def _():
        h_scr[0] = _mlp_norm(x_ref[...], mod_ref, g_ref)

    def step(first, last):
        dh = wu0_ref.shape[0]
        a = (_dot(h_scr[cur, :, :dh], wu0_ref[...])
             + _dot(h_scr[cur, :, dh:], wu1_ref[...]))
        r = jnp.square(jnp.maximum(a, 0.0)).astype(BF16)
        th = wd0_ref.shape[0]
        y = _dot(r[:, :th], wd0_ref[...]) + _dot(r[:, th:], wd1_ref[...])
        if first:
            o_ref[...] = y
        elif not last:
            o_ref[...] += y
        else:
            xo = x_ref[...] + mod_ref[0, 5:6, :] * (o_ref[...] + y)
            if final_norm:
                xo = _rms(xo) * gf_ref[...]
            o_ref[...] = xo
        r0 = pl.multiple_of(f * rows_per_step, rows_per_step)
        h_scr[1 - cur, pl.ds(r0, rows_per_step), :] = _mlp_norm(
            xn_ref[pl.ds(r0, rows_per_step), :], modn_ref, g_ref)

    pl.when(f == 0)(functools.partial(step, True, False))
    pl.when((f > 0) & (f < nf - 1))(functools.partial(step, False, False))
    pl.when(f == nf - 1)(functools.partial(step, False, True))


def _mlp(x2, mod_all, g_all, gf, wu_all, wd_all, layer, seq, final_norm, tm=512,
         tf=MLP_TF):
    t, d = x2.shape
    nf = wu_all.shape[2] // tf
    assert nf >= 2, "the first and last hidden tiles take different code paths"
    per_batch = seq // tm
    last_blk = t // tm - 1
    nxt = lambda i: jnp.minimum(i + 1, last_blk)
    return pl.pallas_call(
        functools.partial(_mlp_kernel, final_norm=final_norm),
        out_shape=jax.ShapeDtypeStruct((t, d), F32),
        grid=(t // tm, nf),
        in_specs=[
            pl.BlockSpec((tm, d), lambda i, f: (i, 0)),
            pl.BlockSpec((tm, d), lambda i, f: (nxt(i), 0)),
            pl.BlockSpec((None, 1, N_MOD, d), lambda i, f: (layer, i // per_batch, 0, 0)),
            pl.BlockSpec((None, 1, N_MOD, d),
                         lambda i, f: (layer, nxt(i) // per_batch, 0, 0)),
            pl.BlockSpec((None, 1, d), lambda i, f: (layer, 0, 0)),
            pl.BlockSpec((1, d), lambda i, f: (0, 0)),
            pl.BlockSpec((None, d // 2, tf), lambda i, f: (layer, 0, f)),
            pl.BlockSpec((None, d // 2, tf), lambda i, f: (layer, 1, f)),
            pl.BlockSpec((None, tf // 2, d), lambda i, f: (layer, 2 * f, 0)),
            pl.BlockSpec((None, tf // 2, d), lambda i, f: (layer, 2 * f + 1, 0)),
        ],
        out_specs=pl.BlockSpec((tm, d), lambda i, f: (i, 0)),
        scratch_shapes=[pltpu.VMEM((2, tm, d), BF16)],
        compiler_params=pltpu.CompilerParams(
            dimension_semantics=("arbitrary", "arbitrary"),
            vmem_limit_bytes=VMEM_LIMIT),
        name="mlp",
    )(x2, x2, mod_all, mod_all, g_all, gf, wu_all, wu_all, wd_all, wd_all)


def _tables(seq):
    pos = jnp.arange(seq, dtype=F32)
    inv_freq = ROPE_BASE ** (-jnp.arange(0, RET_DIM, 2, dtype=F32) / RET_DIM)
    ang = pos[:, None] * inv_freq[None, :]
    cos, sin = jnp.cos(ang), jnp.sin(ang)
    cos2 = jnp.concatenate([cos, cos], axis=-1)
    sin2 = jnp.concatenate([-sin, sin], axis=-1)

    c = RET_CHUNK
    log_g = jnp.log1p(-jnp.exp2(-5.0 - jnp.arange(RET_HEADS, dtype=F32)))
    idx = jnp.arange(c, dtype=F32)
    rel = idx[:, None] - idx[None, :]
    decay = jnp.where(rel[None] >= 0,
                      jnp.exp(jnp.maximum(rel, 0.0)[None] * log_g[:, None, None]), 0.0)
    scale = RET_DIM ** -0.5
    dec = decay * scale
    zeta = jnp.exp((c - 1.0 - idx)[None, :] * log_g[:, None])
    zeta = jnp.broadcast_to(zeta[:, :, None], (RET_HEADS, c, RET_DIM))
    xi = jnp.exp((idx + 1.0)[None, :] * log_g[:, None]) * scale
    xi = jnp.broadcast_to(xi[:, :, None], (RET_HEADS, c, RET_DIM))
    gc = jnp.broadcast_to(jnp.exp(c * log_g)[:, None, None], (RET_HEADS, 1, RET_DIM))

    r = jnp.arange(MIX_CHUNK)
    same = (r[:, None] // GLA_CHUNK) == (r[None, :] // GLA_CHUNK)
    ltri = jnp.where(same & (r[:, None] >= r[None, :]), 1.0, 0.0).astype(BF16)
    return cos2, sin2, dec, zeta, xi, gc, ltri


def _pad_key_heads(a):
    lead = a.shape[:-1]
    a = a.reshape(lead + (GLA_HEADS, GLA_K_DIM))
    a = jnp.pad(a, [(0, 0)] * (len(lead) + 1) + [(0, GLA_K_PAD - GLA_K_DIM)])
    return a.reshape(lead + (GLA_KEY_PAD_WIDTH,))


def _key_columns(a):
    pad_rank = [(0, 0)] * (a.ndim - 1) + [(0, LANE - GLA_RANK)]
    return jnp.concatenate([
        _pad_key_heads(a[..., SRC_GQ:SRC_GK]),
        _pad_key_heads(a[..., SRC_GK:SRC_GV]),
        jnp.pad(a[..., SRC_GZ:], pad_rank),
    ], axis=-1)


def kernel(x, c, w_ada, b_ada, g_mix, g_mlp, w_in, w_gate_up, b_gate, w_pool, s_pool,
           w_out, w_up, w_down, g_final):
    batch, seq, d = x.shape
    depth = w_ada.shape[0]
    tables = _tables(seq)

    c_pad = jnp.pad(c, ((0, 8 - batch), (0, 0)))
    mod_all = _ada(c_pad, w_ada, b_ada)[:, :batch].reshape(depth, batch, N_MOD, d)

    w_in_b = w_in.astype(BF16)
    w_keys_b = _key_columns(w_in).astype(BF16)
    w_out_b = w_out.astype(BF16)
    w_up_b = w_up.astype(BF16)
    w_down_b = w_down.astype(BF16)
    wg_b = jnp.pad(_pad_key_heads(w_gate_up),
                   ((0, 0), (0, LANE - GLA_RANK), (0, 0))).astype(BF16)
    wp_b = w_pool.astype(BF16)
    gf = g_final.reshape(1, d)
    g_mix3 = g_mix.reshape(depth, 1, d)
    g_mlp3 = g_mlp.reshape(depth, 1, d)
    bg3 = _pad_key_heads(b_gate).reshape(depth, 1, -1)
    sp3 = s_pool.reshape(depth, 1, -1)

    x2 = x.reshape(batch * seq, d)
    for l in range(depth):
        mix = _inproj_mixer(x2, mod_all, g_mix3, w_in_b, w_keys_b, tables, wg_b, bg3, wp_b,
                            sp3, l, seq)
        x2 = _outproj(mix, x2, mod_all, w_out_b, l, seq)
        x2 = _mlp(x2, mod_all, g_mlp3, gf, w_up_b, w_down_b, l, seq,
                  final_norm=(l == depth - 1))
    return x2.reshape(batch, seq, d)
```
